```python
import jax, jax.numpy as jnp
from jax import lax
import numpy as np

D_MODEL = 1024
BATCH = 16
SEQ = 2048
DEPTH = 2

N_MIXERS = 2
N_MLA_LAYERS = (DEPTH + 1) // 2
N_HGRN_LAYERS = DEPTH // 2

MLA_HEADS = D_MODEL // 128
MLA_Q_LORA = 3 * D_MODEL // 8
MLA_KV_LORA = D_MODEL // 4
MLA_NOPE = 128
MLA_ROPE = 64
MLA_V = 128
ROPE_THETA = 10000.0
Q_BLOCK = 128

HGRN_EXPAND = 128
HGRN_HEADS = D_MODEL // HGRN_EXPAND
HGRN_F_DIM = HGRN_HEADS * HGRN_EXPAND
HGRN_I_HEAD = D_MODEL // HGRN_HEADS
CHUNK = 64

D_FF = ((8 * D_MODEL // 3 + 127) // 128) * 128
CONV_WIDTH = 3

EPS = 1e-6

kernel_name = "hybrid_mla_hgrn2_convffn_encoder"


def rms_norm(x, gain):
    xf = x.astype(jnp.float32)
    y = xf * lax.rsqrt(jnp.mean(xf * xf, axis=-1, keepdims=True) + EPS)
    return (y * gain.astype(jnp.float32)).astype(x.dtype)


def rope_tables(positions):
    inv_freq = 1.0 / (ROPE_THETA ** (jnp.arange(0, MLA_ROPE, 2, dtype=jnp.float32) / MLA_ROPE))
    ang = positions.astype(jnp.float32)[..., None] * inv_freq
    return jnp.cos(ang), jnp.sin(ang)


def apply_rope(x, cos, sin):
    x1, x2 = jnp.split(x, 2, axis=-1)
    return jnp.concatenate([x1 * cos - x2 * sin, x2 * cos + x1 * sin], axis=-1).astype(x.dtype)


def mla_mixer(h, positions, w_in, q_norm, w_q_up, kv_norm, w_kv_up, w_out):
    B, S, _ = h.shape
    proj = h @ w_in
    c_q, c_kv, k_rope = jnp.split(proj, [MLA_Q_LORA, MLA_Q_LORA + MLA_KV_LORA], axis=-1)
    q = (rms_norm(c_q, q_norm) @ w_q_up).reshape(B, S, MLA_HEADS, MLA_NOPE + MLA_ROPE)
    q_nope, q_rope = q[..., :MLA_NOPE], q[..., MLA_NOPE:]
    kv = (rms_norm(c_kv, kv_norm) @ w_kv_up).reshape(B, S, MLA_HEADS, MLA_NOPE + MLA_V)
    k_nope, v = kv[..., :MLA_NOPE], kv[..., MLA_NOPE:]
    cos, sin = rope_tables(positions)
    q_rope = apply_rope(q_rope, cos[:, :, None, :], sin[:, :, None, :])
    k_rope = apply_rope(k_rope, cos, sin)
    scale = (MLA_NOPE + MLA_ROPE) ** -0.5
    n_blk = S // Q_BLOCK

    def to_blocks(t):
        return jnp.moveaxis(t.reshape(B, n_blk, Q_BLOCK, *t.shape[2:]), 1, 0)

    def attend(blk):
        qn, qr = blk
        s = (jnp.einsum('bqhd,bkhd->bhqk', qn, k_nope)
             + jnp.einsum('bqhr,bkr->bhqk', qr, k_rope))
        p = jax.nn.softmax(s.astype(jnp.float32) * scale, axis=-1).astype(v.dtype)
        return jnp.einsum('bhqk,bkhd->bqhd', p, v)

    o = lax.map(attend, (to_blocks(q_nope), to_blocks(q_rope)))
    o = jnp.moveaxis(o, 0, 1).reshape(B, S, MLA_HEADS * MLA_V)
    return o @ w_out


def chunk_scan(q, k, v, log_f):
    B, H, S, dk = q.shape
    dv = v.shape[-1]
    n = S // CHUNK

    def chunks(t):
        return jnp.moveaxis(t.astype(jnp.float32).reshape(B, H, n, CHUNK, t.shape[-1]), 2, 0)

    lower = jnp.tril(jnp.ones((CHUNK, CHUNK), dtype=bool))[..., None]

    def step(state, xs):
        qc, kc, vc, gc = xs
        b = jnp.cumsum(gc, axis=2)
        o_inter = jnp.einsum('bhtk,bhkv->bhtv', qc * jnp.exp(b), state)
        diff = b[:, :, :, None, :] - b[:, :, None, :, :]
        decay = jnp.exp(jnp.where(lower, diff, -jnp.inf))
        scores = jnp.einsum('bhtk,bhsk,bhtsk->bhts', qc, kc, decay)
        o_intra = jnp.einsum('bhts,bhsv->bhtv', scores, vc)
        b_last = b[:, :, -1, :]
        state = (jnp.exp(b_last)[..., None] * state
                 + jnp.einsum('bhsk,bhsv->bhkv', kc * jnp.exp(b_last[:, :, None, :] - b), vc))
        return state, o_inter + o_intra

    s0 = jnp.zeros((B, H, dk, dv), jnp.float32)
    _, o = lax.scan(step, s0, (chunks(q), chunks(k), chunks(v), chunks(log_f)))
    return jnp.moveaxis(o, 0, 2).reshape(B, H, S, dv)


def hgrn2_mixer(h, layer_idx, w_in, lb_logits, out_norm, w_out):
    B, S, _ = h.shape
    proj = h @ w_in
    q, f_fw, f_bw, i, g = jnp.split(
        proj, [HGRN_F_DIM, 2 * HGRN_F_DIM, 3 * HGRN_F_DIM, 3 * HGRN_F_DIM + D_MODEL], axis=-1)
    probs = jax.nn.softmax(lb_logits.astype(jnp.float32), axis=1)
    lb = (jnp.cumsum(probs, axis=1) - probs[:, :1])[:, layer_idx]

    def heads(t):
        return t.reshape(B, S, HGRN_HEADS, -1).transpose(0, 2, 1, 3)

    q = heads(jax.nn.silu(q)) * HGRN_EXPAND ** -0.5
    i = heads(i)

    def gates(f_raw, lb_d):
        xf = f_raw.astype(jnp.float32)
        log_f = jnp.logaddexp(jnp.log(lb_d), jnp.log1p(-lb_d) + jax.nn.log_sigmoid(xf))
        k = (1.0 - lb_d) * jax.nn.sigmoid(-xf)
        return heads(log_f), heads(k)

    log_f_fw, k_fw = gates(f_fw, lb[0])
    log_f_bw, k_bw = gates(f_bw, lb[1])
    o_fw = chunk_scan(q, k_fw, i, log_f_fw)
    flip = lambda t: jnp.flip(t, axis=2)
    o_bw = flip(chunk_scan(flip(q), flip(k_bw), flip(i), flip(log_f_bw)))
    o = (o_fw + o_bw).transpose(0, 2, 1, 3)
    o = rms_norm(o, out_norm.reshape(HGRN_HEADS, HGRN_I_HEAD))
    o = o.reshape(B, S, D_MODEL).astype(h.dtype) * jax.nn.silu(g)
    return o @ w_out


def conv_ffn(h, w_in, conv_w, conv_b, w_out):
    gate, val = jnp.split(h @ w_in, 2, axis=-1)
    pad = (CONV_WIDTH - 1) // 2
    gate = lax.conv_general_dilated(
        gate, conv_w[:, None, :].astype(gate.dtype), window_strides=(1,),
        padding=[(pad, pad)], dimension_numbers=('NWC', 'WIO', 'NWC'),
        feature_group_count=D_FF) + conv_b
    return (jax.nn.gelu(gate, approximate=True) * val) @ w_out


def _dense(key, shape, fan_in):
    return jax.random.normal(key, shape, jnp.float32) * fan_in ** -0.5


def _gain(key, shape):
    return 1.0 + 0.02 * jax.random.normal(key, shape, jnp.float32)


def setup_inputs(seed: int = 0) -> dict:
    key = jax.random.key(seed)
    ks = jax.random.split(key, 21)
    mla_in = MLA_Q_LORA + MLA_KV_LORA + MLA_ROPE
    return {
        "x": jax.random.normal(ks[0], (BATCH, SEQ, D_MODEL), jnp.float32),
        "positions": (jnp.arange(SEQ, dtype=jnp.int32)[None, :]
                      + jax.random.randint(ks[1], (BATCH, 1), 0, 1024, dtype=jnp.int32)),
        "pre_mix_norm": _gain(ks[2], (DEPTH, D_MODEL)),
        "post_mix_norm": _gain(ks[3], (DEPTH, D_MODEL)),
        "pre_ffn_norm": _gain(ks[4], (DEPTH, D_MODEL)),
        "post_ffn_norm": _gain(ks[5], (DEPTH, D_MODEL)),
        "mla_w_in": _dense(ks[6], (N_MLA_LAYERS, D_MODEL, mla_in), D_MODEL),
        "mla_q_norm": _gain(ks[7], (N_MLA_LAYERS, MLA_Q_LORA)),
        "mla_w_q_up": _dense(ks[8], (N_MLA_LAYERS, MLA_Q_LORA, MLA_HEADS * (MLA_NOPE + MLA_ROPE)), MLA_Q_LORA),
        "mla_kv_norm": _gain(ks[9], (N_MLA_LAYERS, MLA_KV_LORA)),
        "mla_w_kv_up": _dense(ks[10], (N_MLA_LAYERS, MLA_KV_LORA, MLA_HEADS * (MLA_NOPE + MLA_V)), MLA_KV_LORA),
        "mla_w_out": _dense(ks[11], (N_MLA_LAYERS, MLA_HEADS * MLA_V, D_MODEL), MLA_HEADS * MLA_V),
        "hgrn_w_in": _dense(ks[12], (N_HGRN_LAYERS, D_MODEL, 3 * HGRN_F_DIM + 2 * D_MODEL), D_MODEL),
        "hgrn_lb_logits": jax.random.normal(ks[13], (2, DEPTH, HGRN_F_DIM), jnp.float32),
        "hgrn_out_norm": _gain(ks[14], (N_HGRN_LAYERS, D_MODEL)),
        "hgrn_w_out": _dense(ks[15], (N_HGRN_LAYERS, D_MODEL, D_MODEL), D_MODEL),
        "ffn_w_in": _dense(ks[16], (DEPTH, D_MODEL, 2 * D_FF), D_MODEL),
        "ffn_conv_w": _dense(ks[17], (DEPTH, CONV_WIDTH, D_FF), CONV_WIDTH),
        "ffn_conv_b": 0.02 * jax.random.normal(ks[18], (DEPTH, D_FF), jnp.float32),
        "ffn_w_out": _dense(ks[19], (DEPTH, D_FF, D_MODEL), D_FF),
    }


def reference(x, positions, pre_mix_norm, post_mix_norm, pre_ffn_norm, post_ffn_norm,
              mla_w_in, mla_q_norm, mla_w_q_up, mla_kv_norm, mla_w_kv_up, mla_w_out,
              hgrn_w_in, hgrn_lb_logits, hgrn_out_norm, hgrn_w_out,
              ffn_w_in, ffn_conv_w, ffn_conv_b, ffn_w_out):
    for l in range(DEPTH):
        hn = rms_norm(x, pre_mix_norm[l])
        j = l // N_MIXERS
        if l % N_MIXERS == 0:
            m = mla_mixer(hn, positions, mla_w_in[j], mla_q_norm[j], mla_w_q_up[j],
                          mla_kv_norm[j], mla_w_kv_up[j], mla_w_out[j])
        else:
            m = hgrn2_mixer(hn, l, hgrn_w_in[j], hgrn_lb_logits, hgrn_out_norm[j], hgrn_w_out[j])
        x = x + rms_norm(m, post_mix_norm[l])
        hn = rms_norm(x, pre_ffn_norm[l])
        f = conv_ffn(hn, ffn_w_in[l], ffn_conv_w[l], ffn_conv_b[l], ffn_w_out[l])
        x = x + rms_norm(f, post_ffn_norm[l])
    return x
```

```python
import functools

import jax
import jax.numpy as jnp
from jax import lax
from jax.experimental import pallas as pl
from jax.experimental.pallas import tpu as pltpu

F32 = jnp.float32
BF16 = jnp.bfloat16

EPS = 1e-6
ROPE_THETA = 10000.0
MLA_NOPE = 128
MLA_ROPE = 64
MLA_V = 128
HGRN_EXPAND = 128
CONV_WIDTH = 3
SCAN_CHUNK = 64
FFN_COL_CHUNK = 256

V7X_LANES = 128
V7X_SUBLANES = 8
V7X_BF16_ROWS = 16
V7X_VMEM_LIMIT = 56 * 1024 * 1024


def _rms(x, gain):
    ms = jnp.mean(x * x, axis=-1, keepdims=True)
    return x * lax.rsqrt(ms + EPS) * gain


def _sigmoid_pair(x):
    e = jnp.exp(-jnp.abs(x))
    r = 1.0 / (1.0 + e)
    er = e * r
    pos = x >= 0
    return jnp.where(pos, r, er), jnp.where(pos, er, r)


def _dot(a, b):
    return jnp.dot(a, b, preferred_element_type=F32)


def _dot_nt(a, b):
    return lax.dot_general(a, b, (((1,), (1,)), ((), ())), preferred_element_type=F32)


def _dot_tn(a, b):
    return lax.dot_general(a, b, (((0,), (0,)), ((), ())), preferred_element_type=F32)


def _params(*sem):
    return pltpu.CompilerParams(dimension_semantics=sem, vmem_limit_bytes=V7X_VMEM_LIMIT)


def _const_spec(shape):
    nd = len(shape)
    return pl.BlockSpec(shape, lambda *_: (0,) * nd)


def _mla_proj_kernel(x_ref, pos_ref, g_pre_ref, w_in_ref, g_q_ref, w_q_ref, g_kv_ref,
                     w_kv_ref, invf_ref, sgn_ref, q_ref, k_ref, v_ref, *, q_lora, kv_lora, heads):
    hn = _rms(x_ref[...], g_pre_ref[...]).astype(BF16)
    proj = _dot(hn, w_in_ref[...])
    c_q = proj[:, :q_lora]
    c_kv = proj[:, q_lora:q_lora + kv_lora]
    kr = proj[:, q_lora + kv_lora:q_lora + kv_lora + MLA_ROPE]
    kr_sw = proj[:, q_lora + kv_lora + MLA_ROPE:]

    ang = pos_ref[...].astype(F32) * invf_ref[...]
    cos2 = jnp.cos(ang)
    sin2 = jnp.sin(ang) * sgn_ref[...]

    q_all = _dot(_rms(c_q, g_q_ref[...]).astype(BF16), w_q_ref[...])
    kv = _dot(_rms(c_kv, g_kv_ref[...]).astype(BF16), w_kv_ref[...])

    n_nope = heads * MLA_NOPE
    n_rope = heads * MLA_ROPE
    k_rot = (kr * cos2[:, :MLA_ROPE] + kr_sw * sin2[:, :MLA_ROPE]).astype(BF16)
    for pair in range(heads // 2):
        lo = n_nope + pair * V7X_LANES
        q_rot = (q_all[:, lo:lo + V7X_LANES] * cos2
                 + q_all[:, lo + n_rope:lo + n_rope + V7X_LANES] * sin2).astype(BF16)
        for sub in range(2):
            h = 2 * pair + sub
            q_ref[h, :, MLA_NOPE:] = q_rot[:, sub * MLA_ROPE:(sub + 1) * MLA_ROPE]
    for h in range(heads):
        q_ref[h, :, :MLA_NOPE] = q_all[:, h * MLA_NOPE:(h + 1) * MLA_NOPE].astype(BF16)
        k_ref[h, :, :MLA_NOPE] = kv[:, h * MLA_NOPE:(h + 1) * MLA_NOPE].astype(BF16)
        k_ref[h, :, MLA_NOPE:] = k_rot
        v_ref[h] = kv[:, n_nope + h * MLA_V:n_nope + (h + 1) * MLA_V].astype(BF16)


def _mla_proj(x, positions, g_pre, w_in, g_q, w_q_up, g_kv, w_kv_up, *, tm=256):
    B, S, D = x.shape
    q_lora = g_q.shape[0]
    kv_lora = g_kv.shape[0]
    heads = w_q_up.shape[1] // (MLA_NOPE + MLA_ROPE)
    half = MLA_ROPE // 2
    qk_dim = MLA_NOPE + MLA_ROPE

    kr_lo = q_lora + kv_lora
    w_in_x = jnp.concatenate(
        [w_in, w_in[:, kr_lo + half:kr_lo + MLA_ROPE], w_in[:, kr_lo:kr_lo + half]], axis=1).astype(BF16)
    wq3 = w_q_up.reshape(q_lora, heads, qk_dim)
    rope = wq3[:, :, MLA_NOPE:]
    rope_sw = jnp.concatenate([rope[:, :, half:], rope[:, :, :half]], axis=-1)
    w_q_x = jnp.concatenate(
        [wq3[:, :, :MLA_NOPE].reshape(q_lora, -1), rope.reshape(q_lora, -1),
         rope_sw.reshape(q_lora, -1)], axis=1).astype(BF16)
    wkv3 = w_kv_up.reshape(kv_lora, heads, MLA_NOPE + MLA_V)
    w_kv_x = jnp.concatenate(
        [wkv3[:, :, :MLA_NOPE].reshape(kv_lora, -1), wkv3[:, :, MLA_NOPE:].reshape(kv_lora, -1)],
        axis=1).astype(BF16)

    inv_freq = 1.0 / (ROPE_THETA ** (jnp.arange(0, MLA_ROPE, 2, dtype=F32) / MLA_ROPE))
    invf = jnp.tile(inv_freq, V7X_LANES // half)[None, :]
    sgn = jnp.tile(jnp.concatenate([-jnp.ones((half,), F32), jnp.ones((half,), F32)]),
                   V7X_LANES // MLA_ROPE)[None, :]
    scale = float(qk_dim) ** -0.5

    kern = functools.partial(_mla_proj_kernel, q_lora=q_lora, kv_lora=kv_lora, heads=heads)
    head_out = lambda width: pl.BlockSpec((None, heads, tm, width), lambda b, i: (b, 0, i, 0))
    return pl.pallas_call(
        kern,
        grid=(B, S // tm),
        in_specs=[
            pl.BlockSpec((None, tm, D), lambda b, i: (b, i, 0)),
            pl.BlockSpec((None, tm, 1), lambda b, i: (b, i, 0)),
            _const_spec((1, D)),
            _const_spec(w_in_x.shape),
            _const_spec((1, q_lora)),
            _const_spec(w_q_x.shape),
            _const_spec((1, kv_lora)),
            _const_spec(w_kv_x.shape),
            _const_spec((1, V7X_LANES)),
            _const_spec((1, V7X_LANES)),
        ],
        out_specs=[head_out(qk_dim), head_out(qk_dim), head_out(MLA_V)],
        out_shape=[
            jax.ShapeDtypeStruct((B, heads, S, qk_dim), BF16),
            jax.ShapeDtypeStruct((B, heads, S, qk_dim), BF16),
            jax.ShapeDtypeStruct((B, heads, S, MLA_V), BF16),
        ],
        compiler_params=_params("parallel", "parallel"),
        name="mla_proj",
    )(x, positions.reshape(B, S, 1), g_pre[None, :], w_in_x, (g_q * scale)[None, :], w_q_x,
      g_kv[None, :], w_kv_x, invf, sgn)


def _attn_kernel(q_ref, k_ref, v_ref, o_ref, *, tq):
    def q_tile(i, carry):
        r0 = pl.multiple_of(i * tq, tq)
        s = _dot_nt(q_ref[pl.ds(r0, tq), :], k_ref[...])
        p = jnp.exp(s - jnp.max(s, axis=-1, keepdims=True))
        denom = jnp.sum(p, axis=-1, keepdims=True)
        o = _dot(p.astype(BF16), v_ref[...])
        o_ref[pl.ds(r0, tq), :] = (o / denom).astype(o_ref.dtype)
        return carry

    lax.fori_loop(0, q_ref.shape[0] // tq, q_tile, 0)


def _attention(q, k, v, *, tq=512):
    B, H, S, dk = q.shape
    dv = v.shape[-1]
    per_head = lambda width: pl.BlockSpec((None, None, S, width), lambda b, h: (b, h, 0, 0))
    return pl.pallas_call(
        functools.partial(_attn_kernel, tq=tq),
        grid=(B, H),
        in_specs=[per_head(dk), per_head(dk), per_head(dv)],
        out_specs=pl.BlockSpec((None, S, dv), lambda b, h: (b, 0, h)),
        out_shape=jax.ShapeDtypeStruct((B, S, H * dv), BF16),
        compiler_params=_params("parallel", "parallel"),
        name="mla_attention",
    )(q, k, v)


def _mix_epilogue(a_bf16, x_ref, w_ref, g_post_ref, g_ffn_ref, x_out_ref, hn_out_ref):
    m = _dot(a_bf16, w_ref[...])
    x1 = x_ref[...] + _rms(m, g_post_ref[...])
    x_out_ref[...] = x1
    hn_out_ref[...] = _rms(x1, g_ffn_ref[...]).astype(BF16)


def _mla_out_kernel(o_ref, x_ref, w_ref, g_post_ref, g_ffn_ref, x_out_ref, hn_out_ref):
    _mix_epilogue(o_ref[...], x_ref, w_ref, g_post_ref, g_ffn_ref, x_out_ref, hn_out_ref)


def _hgrn_out_kernel(of_ref, ob_ref, gate_ref, g_head_ref, x_ref, w_ref, g_post_ref, g_ffn_ref,
                     x_out_ref, hn_out_ref):
    heads = of_ref.shape[0]
    cols = []
    for h in range(heads):
        o = of_ref[h] + ob_ref[h]
        gx = gate_ref[h]
        silu_g = gx * _sigmoid_pair(gx)[0]
        cols.append((_rms(o, g_head_ref[h]) * silu_g).astype(BF16))
    _mix_epilogue(jnp.concatenate(cols, axis=1), x_ref, w_ref, g_post_ref, g_ffn_ref,
                  x_out_ref, hn_out_ref)


def _mix_out_call(kern, lead_args, lead_specs, x2d, w_out, g_post, g_ffn, tm, name):
    T, D = x2d.shape
    row = pl.BlockSpec((tm, D), lambda i: (i, 0))
    return pl.pallas_call(
        kern,
        grid=(T // tm,),
        in_specs=lead_specs + [row, _const_spec(w_out.shape), _const_spec((1, D)), _const_spec((1, D))],
        out_specs=[row, row],
        out_shape=[jax.ShapeDtypeStruct((T, D), F32), jax.ShapeDtypeStruct((T, D), BF16)],
        compiler_params=_params("parallel"),
        name=name,
    )(*lead_args, x2d, w_out.astype(BF16), g_post[None, :], g_ffn[None, :])


def _gelu_tanh(x):
    c = 0.7978845608028654
    return 0.5 * x * (1.0 + jnp.tanh(c * (x + 0.044715 * (x * x * x))))


def _ffn_kernel(hn_ref, hprev_ref, hnext_ref, x_ref, wg_ref, wv_ref, cw_ref, cb_ref, wo_ref,
                g_post_ref, g_next_ref, x_out_ref, *rest, tiles_per_seq, emit_next):
    if emit_next:
        hn_out_ref, acc_ref = rest
    else:
        (acc_ref,) = rest
    i = pl.program_id(0)
    tm = hn_ref.shape[0]
    hn = hn_ref[...]
    halo = jnp.concatenate([hprev_ref[...], hnext_ref[...]], axis=0)
    t_in_seq = i % tiles_per_seq
    seq_start = t_in_seq == 0
    seq_end = t_in_seq == tiles_per_seq - 1
    rows = lax.broadcasted_iota(jnp.int32, (tm, 1), 0)
    acc_ref[...] = jnp.zeros_like(acc_ref)

    def col_chunk(c, carry):
        wg = wg_ref[c]
        g = _dot(hn, wg)
        gh = _dot(halo, wg)
        val = _dot(hn, wv_ref[c])
        g_before = jnp.where(seq_start, 0.0, gh[V7X_BF16_ROWS - 1:V7X_BF16_ROWS, :])
        g_after = jnp.where(seq_end, 0.0, gh[V7X_BF16_ROWS:V7X_BF16_ROWS + 1, :])
        g_m1 = jnp.where(rows == 0, g_before, pltpu.roll(g, 1, 0))
        g_p1 = jnp.where(rows == tm - 1, g_after, pltpu.roll(g, tm - 1, 0))
        cw = cw_ref[c]
        conv = g_m1 * cw[0:1, :] + g * cw[1:2, :] + g_p1 * cw[2:3, :] + cb_ref[c]
        act = (_gelu_tanh(conv) * val).astype(BF16)
        acc_ref[...] += _dot(act, wo_ref[c])
        return carry

    lax.fori_loop(0, wg_ref.shape[0], col_chunk, 0)
    x2 = x_ref[...] + _rms(acc_ref[...], g_post_ref[...])
    x_out_ref[...] = x2
    if emit_next:
        hn_out_ref[...] = _rms(x2, g_next_ref[...]).astype(BF16)


def _conv_ffn(hn, x2d, seq_len, w_in, conv_w, conv_b, w_out, g_post, g_next, *, emit_next, tm=512):
    T, D = x2d.shape
    d_ff = w_out.shape[0]
    cw = FFN_COL_CHUNK
    nc = d_ff // cw
    wg = w_in[:, :d_ff].reshape(D, nc, cw).transpose(1, 0, 2).astype(BF16)
    wv = w_in[:, d_ff:].reshape(D, nc, cw).transpose(1, 0, 2).astype(BF16)
    wo = w_out.reshape(nc, cw, D).astype(BF16)
    cwt = conv_w.reshape(CONV_WIDTH, nc, cw).transpose(1, 0, 2)
    cbt = conv_b.reshape(nc, 1, cw)
    halo_rows = V7X_BF16_ROWS
    per_tile = tm // halo_rows
    n_halo = T // halo_rows
    row = pl.BlockSpec((tm, D), lambda i: (i, 0))
    out_specs = [row]
    out_shape = [jax.ShapeDtypeStruct((T, D), F32)]
    if emit_next:
        out_specs.append(row)
        out_shape.append(jax.ShapeDtypeStruct((T, D), BF16))
    kern = functools.partial(_ffn_kernel, tiles_per_seq=seq_len // tm, emit_next=emit_next)
    return pl.pallas_call(
        kern,
        grid=(T // tm,),
        in_specs=[
            row,
            pl.BlockSpec((halo_rows, D), lambda i: (jnp.maximum(i * per_tile - 1, 0), 0)),
            pl.BlockSpec((halo_rows, D), lambda i: (jnp.minimum((i + 1) * per_tile, n_halo - 1), 0)),
            row,
            _const_spec(wg.shape), _const_spec(wv.shape), _const_spec(cwt.shape),
            _const_spec(cbt.shape), _const_spec(wo.shape),
            _const_spec((1, D)), _const_spec((1, D)),
        ],
        out_specs=out_specs,
        out_shape=out_shape,
        scratch_shapes=[pltpu.VMEM((tm, D), F32)],
        compiler_params=_params("parallel"),
        name="conv_ffn",
    )(hn, hn, hn, x2d, wg, wv, cwt, cbt, wo, g_post[None, :], g_next[None, :])


def _hgrn_in_kernel(hn_ref, w_ref, o_ref):
    res = _dot(hn_ref[...], w_ref[...])
    for cb in range(o_ref.shape[0]):
        o_ref[cb] = res[:, cb * V7X_LANES:(cb + 1) * V7X_LANES]


def _hgrn_in_proj(hn, w_in, *, tm=1024, tn=1024):
    T, D = hn.shape
    N = w_in.shape[1]
    return pl.pallas_call(
        _hgrn_in_kernel,
        grid=(N // tn, T // tm),
        in_specs=[pl.BlockSpec((tm, D), lambda n, i: (i, 0)),
                  pl.BlockSpec((D, tn), lambda n, i: (0, n))],
        out_specs=pl.BlockSpec((tn // V7X_LANES, tm, V7X_LANES), lambda n, i: (n, i, 0)),
        out_shape=jax.ShapeDtypeStruct((N // V7X_LANES, T, V7X_LANES), F32),
        compiler_params=_params("parallel", "parallel"),
        name="hgrn_in_proj",
    )(hn, w_in.astype(BF16))


def _scan_one_head(lbl_ref, q_ref, f_ref, v_ref, o_ref, st_ref, d, h, masks, *, rev, layer_idx):
    nb = SCAN_CHUNK // V7X_SUBLANES
    depth = lbl_ref.shape[1]

    logits = [lbl_ref[d, i, h] for i in range(depth)]
    mx = functools.reduce(jnp.maximum, logits)
    es = [jnp.exp(l - mx) for l in logits]
    tot = functools.reduce(lambda a, b: a + b, es)
    picked = es[1:layer_idx + 1]
    lb = functools.reduce(lambda a, b: a + b, picked) / tot if picked else jnp.zeros_like(tot)
    one_m_lb = 1.0 - lb

    sig_f, sig_nf = _sigmoid_pair(f_ref[h])
    f = lb + one_m_lb * sig_f
    k = one_m_lb * sig_nf
    qx = q_ref[h]
    q = qx * _sigmoid_pair(qx)[0] * (float(HGRN_EXPAND) ** -0.5)
    v = v_ref[h]

    sub = lax.broadcasted_iota(jnp.int32, (V7X_SUBLANES, V7X_LANES), 0)
    il = (V7X_SUBLANES - 1 - sub) if rev else sub

    def blocks(a):
        bl = [a[V7X_SUBLANES * j:V7X_SUBLANES * (j + 1), :] for j in range(nb)]
        return bl[::-1] if rev else bl

    def unblocks(bl):
        return jnp.concatenate(bl[::-1] if rev else bl, axis=0)

    def row(x, pos):
        u = (V7X_SUBLANES - 1 - pos) if rev else pos
        return jnp.broadcast_to(x[u:u + 1, :], x.shape)

    def at_prev(x):
        return pltpu.roll(x, (V7X_SUBLANES - 1) if rev else 1, 0)

    def at_next(x):
        return pltpu.roll(x, 1 if rev else (V7X_SUBLANES - 1), 0)

    qb, kb = blocks(q), blocks(k)
    hi1, hi2, hi4 = [(il & c) != 0 for c in (1, 2, 4)]

    def level_operands(F, G):
        qh = unblocks([a * b for a, b in zip(qb, F)]).astype(BF16)
        kh = unblocks([a * b for a, b in zip(kb, G)] if G is not None else kb).astype(BF16)
        return qh, kh

    F = blocks(f)
    s_acc = None
    qh, kh = level_operands(F, None)
    s_acc = jnp.where(masks[0], _dot_nt(qh, kh), 0.0)
    G = [jnp.where(hi1, 1.0, at_next(x)) for x in F]
    F = [x * jnp.where(hi1, at_prev(x), 1.0) for x in F]
    qh, kh = level_operands(F, G)
    s_acc = s_acc + jnp.where(masks[1], _dot_nt(qh, kh), 0.0)
    G = [g * jnp.where(hi2, 1.0, jnp.where(hi4, row(x, 7), row(x, 3))) for x, g in zip(F, G)]
    F = [x * jnp.where(hi2, jnp.where(hi4, row(x, 5), row(x, 1)), 1.0) for x in F]
    qh, kh = level_operands(F, G)
    s_acc = s_acc + jnp.where(masks[2], _dot_nt(qh, kh), 0.0)
    G = [g * jnp.where(hi4, 1.0, row(x, 7)) for x, g in zip(F, G)]
    F = [x * jnp.where(hi4, row(x, 3), 1.0) for x in F]
    last = V7X_SUBLANES - 1
    for lvl, cb in enumerate((1, 2, 4)):
        qh, kh = level_operands(F, G)
        s_acc = s_acc + jnp.where(masks[3 + lvl], _dot_nt(qh, kh), 0.0)
        newF, newG = [], []
        for b in range(nb):
            base = b & ~(2 * cb - 1)
            if b & cb:
                newF.append(F[b] * row(F[base + cb - 1], last))
                newG.append(G[b])
            else:
                newF.append(F[b])
                newG.append(G[b] * row(F[base + 2 * cb - 1], last))
        F, G = newF, newG

    q_dec, k_dec = level_operands(F, G)
    st = st_ref[d, h]
    v16 = v.astype(BF16)
    diag = jnp.sum(q * k, axis=-1, keepdims=True)
    o = _dot_nt(q_dec, st.astype(BF16)) + _dot(s_acc.astype(BF16), v16) + diag * v
    o_ref[h] = o
    f_total = row(F[nb - 1], last)[0:1, :]
    st_ref[d, h] = st * f_total + _dot_tn(v16, k_dec)


def _hgrn_scan_kernel(lbl_ref, qf_ref, ff_ref, vf_ref, qb_ref, fb_ref, vb_ref, of_ref, ob_ref,
                      st_ref, *, layer_idx):
    @pl.when(pl.program_id(1) == 0)
    def _():
        st_ref[...] = jnp.zeros_like(st_ref)

    heads = qf_ref.shape[0]
    n = SCAN_CHUNK
    r = lax.broadcasted_iota(jnp.int32, (n, n), 0)
    s = lax.broadcasted_iota(jnp.int32, (n, n), 1)

    def level_masks(rev):
        ti = (n - 1 - r) if rev else r
        si = (n - 1 - s) if rev else s
        out = []
        c = 1
        while c < n:
            same_block = (ti // (2 * c)) == (si // (2 * c))
            out.append(((ti & c) != 0) & ((si & c) == 0) & same_block)
            c *= 2
        return out

    for d, (q_ref, f_ref, v_ref, o_ref) in enumerate(
            ((qf_ref, ff_ref, vf_ref, of_ref), (qb_ref, fb_ref, vb_ref, ob_ref))):
        rev = d == 1
        masks = level_masks(rev)

        def head(h, carry, q_ref=q_ref, f_ref=f_ref, v_ref=v_ref, o_ref=o_ref, masks=masks,
                 rev=rev, d=d):
            _scan_one_head(lbl_ref, q_ref, f_ref, v_ref, o_ref, st_ref, d, h, masks,
                           rev=rev, layer_idx=layer_idx)
            return carry

        lax.fori_loop(0, heads, head, 0)


def _hgrn_scan(proj, lb_logits, *, batch, seq_len, layer_idx, heads):
    T = proj.shape[1]
    hd = proj.shape[2]
    nchunk = seq_len // SCAN_CHUNK
    p5 = proj.reshape(5, heads, batch, seq_len, hd)
    depth = lb_logits.shape[1]
    lbl = lb_logits.reshape(2, depth, heads, 1, hd)

    def sec(section, backward):
        if backward:
            return pl.BlockSpec((None, heads, None, SCAN_CHUNK, hd),
                                lambda b, j: (section, 0, b, nchunk - 1 - j, 0))
        return pl.BlockSpec((None, heads, None, SCAN_CHUNK, hd), lambda b, j: (section, 0, b, j, 0))

    out_f = pl.BlockSpec((heads, None, SCAN_CHUNK, hd), lambda b, j: (0, b, j, 0))
    out_b = pl.BlockSpec((heads, None, SCAN_CHUNK, hd), lambda b, j: (0, b, nchunk - 1 - j, 0))
    o_sds = jax.ShapeDtypeStruct((heads, batch, seq_len, hd), F32)
    of, ob = pl.pallas_call(
        functools.partial(_hgrn_scan_kernel, layer_idx=layer_idx),
        grid=(batch, nchunk),
        in_specs=[_const_spec(lbl.shape),
                  sec(0, False), sec(1, False), sec(3, False),
                  sec(0, True), sec(2, True), sec(3, True)],
        out_specs=[out_f, out_b],
        out_shape=[o_sds, o_sds],
        scratch_shapes=[pltpu.VMEM((2, heads, hd, hd), F32)],
        compiler_params=_params("arbitrary", "arbitrary"),
        name="hgrn_scan",
    )(lbl, p5, p5, p5, p5, p5, p5)
    return of.reshape(heads, T, hd), ob.reshape(heads, T, hd)


def kernel(x, positions, pre_mix_norm, post_mix_norm, pre_ffn_norm, post_ffn_norm, mla_w_in, mla_q_norm, mla_w_q_up, mla_kv_norm, mla_w_kv_up, mla_w_out, hgrn_w_in, hgrn_lb_logits, hgrn_out_norm, hgrn_w_out, ffn_w_in, ffn_conv_w, ffn_conv_b, ffn_w_out):
    B, S, D = x.shape
    T = B * S
    depth = pre_mix_norm.shape[0]
    n_mixers = 2
    hgrn_heads = D // HGRN_EXPAND
    x2d = x.reshape(T, D)
    hn = None
    tm_out = 512
    for l in range(depth):
        j = l // n_mixers
        if l % n_mixers == 0:
            q, k, v = _mla_proj(x2d.reshape(B, S, D), positions, pre_mix_norm[l], mla_w_in[j],
                                mla_q_norm[j], mla_w_q_up[j], mla_kv_norm[j], mla_w_kv_up[j])
            o = _attention(q, k, v).reshape(T, -1)
            x2d, hn = _mix_out_call(
                _mla_out_kernel, [o], [pl.BlockSpec((tm_out, o.shape[1]), lambda i: (i, 0))],
                x2d, mla_w_out[j], post_mix_norm[l], pre_ffn_norm[l], tm_out, "mla_out")
        else:
            proj = _hgrn_in_proj(hn, hgrn_w_in[j])
            of, ob = _hgrn_scan(proj, hgrn_lb_logits, batch=B, seq_len=S, layer_idx=l,
                                heads=hgrn_heads)
            hd = proj.shape[2]
            head_rows = pl.BlockSpec((hgrn_heads, tm_out, hd), lambda i: (0, i, 0))
            gate_rows = pl.BlockSpec((hgrn_heads, tm_out, hd), lambda i: (4, i, 0))
            x2d, hn = _mix_out_call(
                _hgrn_out_kernel,
                [of, ob, proj, hgrn_out_norm[j].reshape(hgrn_heads, 1, hd)],
                [head_rows, head_rows, gate_rows, _const_spec((hgrn_heads, 1, hd))],
                x2d, hgrn_w_out[j], post_mix_norm[l], pre_ffn_norm[l], tm_out, "hgrn_out")
        last = l == depth - 1
        g_next = pre_mix_norm[l] if last else pre_mix_norm[l + 1]
        outs = _conv_ffn(hn, x2d, S, ffn_w_in[l], ffn_conv_w[l], ffn_conv_b[l], ffn_w_out[l],
                         post_ffn_norm[l], g_next, emit_next=not last)
        if last:
            (x2d,) = outs
        else:
            x2d, hn = outs
    return x2d.reshape(B, S, D)
```

```python
import functools

import jax
import jax.numpy as jnp
import numpy as np
from jax import lax
from jax.experimental import pallas as pl
from jax.experimental.pallas import tpu as pltpu

F32 = jnp.float32
BF16 = jnp.bfloat16

EPS = 1e-6
ROPE_THETA = 10000.0
MLA_NOPE = 128
MLA_ROPE = 64
MLA_V = 128
HGRN_EXPAND = 128
CONV_WIDTH = 3
SCAN_CHUNK = 64
FFN_COL_CHUNK = 256

V7X_LANES = 128
V7X_SUBLANES = 8
V7X_BF16_ROWS = 16
V7X_VMEM_LIMIT = 56 * 1024 * 1024


def _rms(x, gain):
    ms = jnp.mean(x * x, axis=-1, keepdims=True)
    return x * lax.rsqrt(ms + EPS) * gain


def _sigmoid_pair(x):
    e = jnp.exp(-jnp.abs(x))
    r = 1.0 / (1.0 + e)
    er = e * r
    pos = x >= 0
    return jnp.where(pos, r, er), jnp.where(pos, er, r)


def _dot(a, b):
    return jnp.dot(a, b, preferred_element_type=F32)


def _dot_nt(a, b):
    return lax.dot_general(a, b, (((1,), (1,)), ((), ())), preferred_element_type=F32)


def _dot_tn(a, b):
    return lax.dot_general(a, b, (((0,), (0,)), ((), ())), preferred_element_type=F32)


def _params(*sem):
    return pltpu.CompilerParams(dimension_semantics=sem, vmem_limit_bytes=V7X_VMEM_LIMIT)


def _const_spec(shape):
    nd = len(shape)
    return pl.BlockSpec(shape, lambda *_: (0,) * nd)


def _mla_proj_kernel(x_ref, pos_ref, g_pre_ref, w_in_ref, g_q_ref, w_q_ref, g_kv_ref,
                     w_kv_ref, invf_ref, sgn_ref, q_ref, k_ref, v_ref, *, q_lora, kv_lora, heads):
    hn = _rms(x_ref[...], g_pre_ref[...]).astype(BF16)
    proj = _dot(hn, w_in_ref[...])
    c_q = proj[:, :q_lora]
    c_kv = proj[:, q_lora:q_lora + kv_lora]
    kr = proj[:, q_lora + kv_lora:q_lora + kv_lora + MLA_ROPE]
    kr_sw = proj[:, q_lora + kv_lora + MLA_ROPE:]

    ang = pos_ref[...].astype(F32) * invf_ref[...]
    cos2 = jnp.cos(ang)
    sin2 = jnp.sin(ang) * sgn_ref[...]

    q_all = _dot(_rms(c_q, g_q_ref[...]).astype(BF16), w_q_ref[...])
    kv = _dot(_rms(c_kv, g_kv_ref[...]).astype(BF16), w_kv_ref[...])

    n_nope = heads * MLA_NOPE
    n_rope = heads * MLA_ROPE
    k_rot = (kr * cos2[:, :MLA_ROPE] + kr_sw * sin2[:, :MLA_ROPE]).astype(BF16)
    for pair in range(heads // 2):
        lo = n_nope + pair * V7X_LANES
        q_rot = (q_all[:, lo:lo + V7X_LANES] * cos2
                 + q_all[:, lo + n_rope:lo + n_rope + V7X_LANES] * sin2).astype(BF16)
        for sub in range(2):
            h = 2 * pair + sub
            q_ref[h, :, MLA_NOPE:] = q_rot[:, sub * MLA_ROPE:(sub + 1) * MLA_ROPE]
    for h in range(heads):
        q_ref[h, :, :MLA_NOPE] = q_all[:, h * MLA_NOPE:(h + 1) * MLA_NOPE].astype(BF16)
        k_ref[h, :, :MLA_NOPE] = kv[:, h * MLA_NOPE:(h + 1) * MLA_NOPE].astype(BF16)
        k_ref[h, :, MLA_NOPE:] = k_rot
        v_ref[h] = kv[:, n_nope + h * MLA_V:n_nope + (h + 1) * MLA_V].astype(BF16)


def _mla_proj(x, positions, g_pre, w_in, g_q, w_q_up, g_kv, w_kv_up, *, tm=256):
    B, S, D = x.shape
    q_lora = g_q.shape[0]
    kv_lora = g_kv.shape[0]
    heads = w_q_up.shape[1] // (MLA_NOPE + MLA_ROPE)
    half = MLA_ROPE // 2
    qk_dim = MLA_NOPE + MLA_ROPE

    kr_lo = q_lora + kv_lora
    w_in_x = jnp.concatenate(
        [w_in, w_in[:, kr_lo + half:kr_lo + MLA_ROPE], w_in[:, kr_lo:kr_lo + half]], axis=1).astype(BF16)
    wq3 = w_q_up.reshape(q_lora, heads, qk_dim)
    rope = wq3[:, :, MLA_NOPE:]
    rope_sw = jnp.concatenate([rope[:, :, half:], rope[:, :, :half]], axis=-1)
    w_q_x = jnp.concatenate(
        [wq3[:, :, :MLA_NOPE].reshape(q_lora, -1), rope.reshape(q_lora, -1),
         rope_sw.reshape(q_lora, -1)], axis=1).astype(BF16)
    wkv3 = w_kv_up.reshape(kv_lora, heads, MLA_NOPE + MLA_V)
    w_kv_x = jnp.concatenate(
        [wkv3[:, :, :MLA_NOPE].reshape(kv_lora, -1), wkv3[:, :, MLA_NOPE:].reshape(kv_lora, -1)],
        axis=1).astype(BF16)

    inv_freq = 1.0 / (ROPE_THETA ** (jnp.arange(0, MLA_ROPE, 2, dtype=F32) / MLA_ROPE))
    invf = jnp.tile(inv_freq, V7X_LANES // half)[None, :]
    sgn = jnp.tile(jnp.concatenate([-jnp.ones((half,), F32), jnp.ones((half,), F32)]),
                   V7X_LANES // MLA_ROPE)[None, :]
    scale = float(qk_dim) ** -0.5

    kern = functools.partial(_mla_proj_kernel, q_lora=q_lora, kv_lora=kv_lora, heads=heads)
    head_out = lambda width: pl.BlockSpec((None, heads, tm, width), lambda b, i: (b, 0, i, 0))
    return pl.pallas_call(
        kern,
        grid=(B, S // tm),
        in_specs=[
            pl.BlockSpec((None, tm, D), lambda b, i: (b, i, 0)),
            pl.BlockSpec((None, tm, 1), lambda b, i: (b, i, 0)),
            _const_spec((1, D)),
            _const_spec(w_in_x.shape),
            _const_spec((1, q_lora)),
            _const_spec(w_q_x.shape),
            _const_spec((1, kv_lora)),
            _const_spec(w_kv_x.shape),
            _const_spec((1, V7X_LANES)),
            _const_spec((1, V7X_LANES)),
        ],
        out_specs=[head_out(qk_dim), head_out(qk_dim), head_out(MLA_V)],
        out_shape=[
            jax.ShapeDtypeStruct((B, heads, S, qk_dim), BF16),
            jax.ShapeDtypeStruct((B, heads, S, qk_dim), BF16),
            jax.ShapeDtypeStruct((B, heads, S, MLA_V), BF16),
        ],
        compiler_params=_params("parallel", "parallel"),
        name="mla_proj",
    )(x, positions.reshape(B, S, 1), g_pre[None, :], w_in_x, (g_q * scale)[None, :], w_q_x,
      g_kv[None, :], w_kv_x, invf, sgn)


def _attn_kernel(q_ref, k_ref, v_ref, o_ref, *, tq):
    def q_tile(i, carry):
        r0 = pl.multiple_of(i * tq, tq)
        s = _dot_nt(q_ref[pl.ds(r0, tq), :], k_ref[...])
        p = jnp.exp(s - jnp.max(s, axis=-1, keepdims=True))
        denom = jnp.sum(p, axis=-1, keepdims=True)
        o = _dot(p.astype(BF16), v_ref[...])
        o_ref[pl.ds(r0, tq), :] = (o / denom).astype(o_ref.dtype)
        return carry

    lax.fori_loop(0, q_ref.shape[0] // tq, q_tile, 0, unroll=8)


def _attention(q, k, v, *, tq=256):
    B, H, S, dk = q.shape
    dv = v.shape[-1]
    per_head = lambda width: pl.BlockSpec((None, None, S, width), lambda b, h: (b, h, 0, 0))
    return pl.pallas_call(
        functools.partial(_attn_kernel, tq=tq),
        grid=(B, H),
        in_specs=[per_head(dk), per_head(dk), per_head(dv)],
        out_specs=pl.BlockSpec((None, S, dv), lambda b, h: (b, 0, h)),
        out_shape=jax.ShapeDtypeStruct((B, S, H * dv), BF16),
        compiler_params=_params("parallel", "parallel"),
        name="mla_attention",
    )(q, k, v)


def _mix_epilogue(a_bf16, x_ref, w_ref, g_post_ref, g_ffn_ref, x_out_ref, hn_out_ref):
    m = _dot(a_bf16, w_ref[...])
    x1 = x_ref[...] + _rms(m, g_post_ref[...])
    x_out_ref[...] = x1
    hn_out_ref[...] = _rms(x1, g_ffn_ref[...]).astype(BF16)


def _mla_out_kernel(o_ref, x_ref, w_ref, g_post_ref, g_ffn_ref, x_out_ref, hn_out_ref):
    _mix_epilogue(o_ref[...], x_ref, w_ref, g_post_ref, g_ffn_ref, x_out_ref, hn_out_ref)


def _hgrn_out_kernel(of_ref, ob_ref, gate_ref, g_head_ref, x_ref, w_ref, g_post_ref, g_ffn_ref,
                     x_out_ref, hn_out_ref):
    heads = of_ref.shape[0]
    cols = []
    for h in range(heads):
        o = of_ref[h] + ob_ref[h]
        gx = gate_ref[h]
        silu_g = gx * _sigmoid_pair(gx)[0]
        cols.append((_rms(o, g_head_ref[h]) * silu_g).astype(BF16))
    _mix_epilogue(jnp.concatenate(cols, axis=1), x_ref, w_ref, g_post_ref, g_ffn_ref,
                  x_out_ref, hn_out_ref)


def _mix_out_call(kern, lead_args, lead_specs, x2d, w_out, g_post, g_ffn, tm, name):
    T, D = x2d.shape
    row = pl.BlockSpec((tm, D), lambda i: (i, 0))
    return pl.pallas_call(
        kern,
        grid=(T // tm,),
        in_specs=lead_specs + [row, _const_spec(w_out.shape), _const_spec((1, D)), _const_spec((1, D))],
        out_specs=[row, row],
        out_shape=[jax.ShapeDtypeStruct((T, D), F32), jax.ShapeDtypeStruct((T, D), BF16)],
        compiler_params=_params("parallel"),
        name=name,
    )(*lead_args, x2d, w_out.astype(BF16), g_post[None, :], g_ffn[None, :])


def _gelu_tanh(x):
    c = 0.7978845608028654
    return 0.5 * x * (1.0 + jnp.tanh(c * (x + 0.044715 * (x * x * x))))


def _ffn_kernel(hn_ref, hprev_ref, hnext_ref, x_ref, wg_ref, wv_ref, cw_ref, cb_ref, wo_ref,
                g_post_ref, g_next_ref, x_out_ref, *rest, tiles_per_seq, emit_next):
    if emit_next:
        hn_out_ref, acc_ref = rest
    else:
        (acc_ref,) = rest
    i = pl.program_id(0)
    tm = hn_ref.shape[0]
    hn = hn_ref[...]
    halo = jnp.concatenate([hprev_ref[...], hnext_ref[...]], axis=0)
    t_in_seq = i % tiles_per_seq
    seq_start = t_in_seq == 0
    seq_end = t_in_seq == tiles_per_seq - 1
    rows = lax.broadcasted_iota(jnp.int32, (tm, 1), 0)
    acc_ref[...] = jnp.zeros_like(acc_ref)

    def col_chunk(c, carry):
        wg = wg_ref[c]
        g = _dot(hn, wg)
        gh = _dot(halo, wg)
        val = _dot(hn, wv_ref[c])
        g_before = jnp.where(seq_start, 0.0, gh[V7X_BF16_ROWS - 1:V7X_BF16_ROWS, :])
        g_after = jnp.where(seq_end, 0.0, gh[V7X_BF16_ROWS:V7X_BF16_ROWS + 1, :])
        g_m1 = jnp.where(rows == 0, g_before, pltpu.roll(g, 1, 0))
        g_p1 = jnp.where(rows == tm - 1, g_after, pltpu.roll(g, tm - 1, 0))
        cw = cw_ref[c]
        conv = g_m1 * cw[0:1, :] + g * cw[1:2, :] + g_p1 * cw[2:3, :] + cb_ref[c]
        act = (_gelu_tanh(conv) * val).astype(BF16)
        acc_ref[...] += _dot(act, wo_ref[c])
        return carry

    lax.fori_loop(0, wg_ref.shape[0], col_chunk, 0)
    x2 = x_ref[...] + _rms(acc_ref[...], g_post_ref[...])
    x_out_ref[...] = x2
    if emit_next:
        hn_out_ref[...] = _rms(x2, g_next_ref[...]).astype(BF16)


def _conv_ffn(hn, x2d, seq_len, w_in, conv_w, conv_b, w_out, g_post, g_next, *, emit_next, tm=512):
    T, D = x2d.shape
    d_ff = w_out.shape[0]
    cw = FFN_COL_CHUNK
    nc = d_ff // cw
    wg = w_in[:, :d_ff].reshape(D, nc, cw).transpose(1, 0, 2).astype(BF16)
    wv = w_in[:, d_ff:].reshape(D, nc, cw).transpose(1, 0, 2).astype(BF16)
    wo = w_out.reshape(nc, cw, D).astype(BF16)
    cwt = conv_w.reshape(CONV_WIDTH, nc, cw).transpose(1, 0, 2)
    cbt = conv_b.reshape(nc, 1, cw)
    halo_rows = V7X_BF16_ROWS
    per_tile = tm // halo_rows
    n_halo = T // halo_rows
    row = pl.BlockSpec((tm, D), lambda i: (i, 0))
    out_specs = [row]
    out_shape = [jax.ShapeDtypeStruct((T, D), F32)]
    if emit_next:
        out_specs.append(row)
        out_shape.append(jax.ShapeDtypeStruct((T, D), BF16))
    kern = functools.partial(_ffn_kernel, tiles_per_seq=seq_len // tm, emit_next=emit_next)
    return pl.pallas_call(
        kern,
        grid=(T // tm,),
        in_specs=[
            row,
            pl.BlockSpec((halo_rows, D), lambda i: (jnp.maximum(i * per_tile - 1, 0), 0)),
            pl.BlockSpec((halo_rows, D), lambda i: (jnp.minimum((i + 1) * per_tile, n_halo - 1), 0)),
            row,
            _const_spec(wg.shape), _const_spec(wv.shape), _const_spec(cwt.shape),
            _const_spec(cbt.shape), _const_spec(wo.shape),
            _const_spec((1, D)), _const_spec((1, D)),
        ],
        out_specs=out_specs,
        out_shape=out_shape,
        scratch_shapes=[pltpu.VMEM((tm, D), F32)],
        compiler_params=_params("parallel"),
        name="conv_ffn",
    )(hn, hn, hn, x2d, wg, wv, cwt, cbt, wo, g_post[None, :], g_next[None, :])


def _hgrn_in_kernel(hn_ref, w_ref, o_ref):
    res = _dot(hn_ref[...], w_ref[...])
    for cb in range(o_ref.shape[0]):
        o_ref[cb] = res[:, cb * V7X_LANES:(cb + 1) * V7X_LANES]


def _hgrn_in_proj(hn, w_in, *, tm=1024, tn=1024):
    T, D = hn.shape
    N = w_in.shape[1]
    return pl.pallas_call(
        _hgrn_in_kernel,
        grid=(N // tn, T // tm),
        in_specs=[pl.BlockSpec((tm, D), lambda n, i: (i, 0)),
                  pl.BlockSpec((D, tn), lambda n, i: (0, n))],
        out_specs=pl.BlockSpec((tn // V7X_LANES, tm, V7X_LANES), lambda n, i: (n, i, 0)),
        out_shape=jax.ShapeDtypeStruct((N // V7X_LANES, T, V7X_LANES), F32),
        compiler_params=_params("parallel", "parallel"),
        name="hgrn_in_proj",
    )(hn, w_in.astype(BF16))


def _scan_one_head(lbl_ref, mask_ref, q_ref, f_ref, v_ref, o_ref, st_ref, d, h, *, rev, layer_idx):
    nb = SCAN_CHUNK // V7X_SUBLANES
    masks = [mask_ref[d, lvl] for lvl in range(mask_ref.shape[1])]
    depth = lbl_ref.shape[1]

    logits = [lbl_ref[d, i, h] for i in range(depth)]
    mx = functools.reduce(jnp.maximum, logits)
    es = [jnp.exp(l - mx) for l in logits]
    tot = functools.reduce(lambda a, b: a + b, es)
    picked = es[1:layer_idx + 1]
    lb = functools.reduce(lambda a, b: a + b, picked) / tot if picked else jnp.zeros_like(tot)
    one_m_lb = 1.0 - lb

    sig_f, sig_nf = _sigmoid_pair(f_ref[h])
    f = lb + one_m_lb * sig_f
    k = one_m_lb * sig_nf
    qx = q_ref[h]
    q = qx * _sigmoid_pair(qx)[0] * (float(HGRN_EXPAND) ** -0.5)
    v = v_ref[h]

    sub = lax.broadcasted_iota(jnp.int32, (V7X_SUBLANES, V7X_LANES), 0)
    il = (V7X_SUBLANES - 1 - sub) if rev else sub

    def blocks(a):
        bl = [a[V7X_SUBLANES * j:V7X_SUBLANES * (j + 1), :] for j in range(nb)]
        return bl[::-1] if rev else bl

    def unblocks(bl):
        return jnp.concatenate(bl[::-1] if rev else bl, axis=0)

    def row(x, pos):
        u = (V7X_SUBLANES - 1 - pos) if rev else pos
        return jnp.broadcast_to(x[u:u + 1, :], x.shape)

    def at_prev(x):
        return pltpu.roll(x, (V7X_SUBLANES - 1) if rev else 1, 0)

    def at_next(x):
        return pltpu.roll(x, 1 if rev else (V7X_SUBLANES - 1), 0)

    qb, kb = blocks(q), blocks(k)
    hi1, hi2, hi4 = [(il & c) != 0 for c in (1, 2, 4)]

    def level_operands(F, G):
        qh = unblocks([a * b for a, b in zip(qb, F)]).astype(BF16)
        kh = unblocks([a * b for a, b in zip(kb, G)] if G is not None else kb).astype(BF16)
        return qh, kh

    F = blocks(f)
    s_acc = None
    qh, kh = level_operands(F, None)
    s_acc = _dot_nt(qh, kh) * masks[0]
    G = [jnp.where(hi1, 1.0, at_next(x)) for x in F]
    F = [x * jnp.where(hi1, at_prev(x), 1.0) for x in F]
    qh, kh = level_operands(F, G)
    s_acc = s_acc + _dot_nt(qh, kh) * masks[1]
    G = [g * jnp.where(hi2, 1.0, jnp.where(hi4, row(x, 7), row(x, 3))) for x, g in zip(F, G)]
    F = [x * jnp.where(hi2, jnp.where(hi4, row(x, 5), row(x, 1)), 1.0) for x in F]
    qh, kh = level_operands(F, G)
    s_acc = s_acc + _dot_nt(qh, kh) * masks[2]
    G = [g * jnp.where(hi4, 1.0, row(x, 7)) for x, g in zip(F, G)]
    F = [x * jnp.where(hi4, row(x, 3), 1.0) for x in F]
    last = V7X_SUBLANES - 1
    for lvl, cb in enumerate((1, 2, 4)):
        qh, kh = level_operands(F, G)
        s_acc = s_acc + _dot_nt(qh, kh) * masks[3 + lvl]
        newF, newG = [], []
        for b in range(nb):
            base = b & ~(2 * cb - 1)
            if b & cb:
                newF.append(F[b] * row(F[base + cb - 1], last))
                newG.append(G[b])
            else:
                newF.append(F[b])
                newG.append(G[b] * row(F[base + 2 * cb - 1], last))
        F, G = newF, newG

    q_dec, k_dec = level_operands(F, G)
    st = st_ref[d, h]
    v16 = v.astype(BF16)
    diag = jnp.sum(q * k, axis=-1, keepdims=True)
    o = _dot_nt(q_dec, st.astype(BF16)) + _dot(s_acc.astype(BF16), v16) + diag * v
    o_ref[h] = o
    f_total = row(F[nb - 1], last)[0:1, :]
    st_ref[d, h] = st * f_total + _dot_tn(v16, k_dec)


def _scan_level_masks():
    n = SCAN_CHUNK
    pos = np.arange(n)
    out = np.zeros((2, n.bit_length() - 1, n, n), np.float32)
    for d in range(2):
        idx = (n - 1 - pos) if d == 1 else pos
        ti, si = idx[:, None], idx[None, :]
        lvl, c = 0, 1
        while c < n:
            out[d, lvl] = ((ti & c) != 0) & ((si & c) == 0) & ((ti // (2 * c)) == (si // (2 * c)))
            lvl, c = lvl + 1, 2 * c
    return out


def _hgrn_scan_kernel(lbl_ref, mask_ref, qf_ref, ff_ref, vf_ref, qb_ref, fb_ref, vb_ref, of_ref,
                      ob_ref, st_ref, *, layer_idx):
    @pl.when(pl.program_id(1) == 0)
    def _():
        st_ref[...] = jnp.zeros_like(st_ref)

    def head(h, carry):
        _scan_one_head(lbl_ref, mask_ref, qf_ref, ff_ref, vf_ref, of_ref, st_ref, 0, h,
                       rev=False, layer_idx=layer_idx)
        _scan_one_head(lbl_ref, mask_ref, qb_ref, fb_ref, vb_ref, ob_ref, st_ref, 1, h,
                       rev=True, layer_idx=layer_idx)
        return carry

    lax.fori_loop(0, qf_ref.shape[0], head, 0, unroll=8)


def _hgrn_scan(proj, lb_logits, *, batch, seq_len, layer_idx, heads):
    T = proj.shape[1]
    hd = proj.shape[2]
    nchunk = seq_len // SCAN_CHUNK
    p5 = proj.reshape(5, heads, batch, seq_len, hd)
    depth = lb_logits.shape[1]
    lbl = lb_logits.reshape(2, depth, heads, 1, hd)
    masks = jnp.asarray(_scan_level_masks())

    def sec(section, backward):
        if backward:
            return pl.BlockSpec((None, heads, None, SCAN_CHUNK, hd),
                                lambda b, j: (section, 0, b, nchunk - 1 - j, 0))
        return pl.BlockSpec((None, heads, None, SCAN_CHUNK, hd), lambda b, j: (section, 0, b, j, 0))

    out_f = pl.BlockSpec((heads, None, SCAN_CHUNK, hd), lambda b, j: (0, b, j, 0))
    out_b = pl.BlockSpec((heads, None, SCAN_CHUNK, hd), lambda b, j: (0, b, nchunk - 1 - j, 0))
    o_sds = jax.ShapeDtypeStruct((heads, batch, seq_len, hd), F32)
    of, ob = pl.pallas_call(
        functools.partial(_hgrn_scan_kernel, layer_idx=layer_idx),
        grid=(batch, nchunk),
        in_specs=[_const_spec(lbl.shape), _const_spec(masks.shape),
                  sec(0, False), sec(1, False), sec(3, False),
                  sec(0, True), sec(2, True), sec(3, True)],
        out_specs=[out_f, out_b],
        out_shape=[o_sds, o_sds],
        scratch_shapes=[pltpu.VMEM((2, heads, hd, hd), F32)],
        compiler_params=_params("arbitrary", "arbitrary"),
        name="hgrn_scan",
    )(lbl, masks, p5, p5, p5, p5, p5, p5)
    return of.reshape(heads, T, hd), ob.reshape(heads, T, hd)


def kernel(x, positions, pre_mix_norm, post_mix_norm, pre_ffn_norm, post_ffn_norm, mla_w_in, mla_q_norm, mla_w_q_up, mla_kv_norm, mla_w_kv_up, mla_w_out, hgrn_w_in, hgrn_lb_logits, hgrn_out_norm, hgrn_w_out, ffn_w_in, ffn_conv_w, ffn_conv_b, ffn_w_out):
    B, S, D = x.shape
    T = B * S
    depth = pre_mix_norm.shape[0]
    n_mixers = 2
    hgrn_heads = D // HGRN_EXPAND
    x2d = x.reshape(T, D)
    hn = None
    tm_out = 512
    for l in range(depth):
        j = l // n_mixers
        if l % n_mixers == 0:
            q, k, v = _mla_proj(x2d.reshape(B, S, D), positions, pre_mix_norm[l], mla_w_in[j],
                                mla_q_norm[j], mla_w_q_up[j], mla_kv_norm[j], mla_w_kv_up[j])
            o = _attention(q, k, v).reshape(T, -1)
            x2d, hn = _mix_out_call(
                _mla_out_kernel, [o], [pl.BlockSpec((tm_out, o.shape[1]), lambda i: (i, 0))],
                x2d, mla_w_out[j], post_mix_norm[l], pre_ffn_norm[l], tm_out, "mla_out")
        else:
            proj = _hgrn_in_proj(hn, hgrn_w_in[j])
            of, ob = _hgrn_scan(proj, hgrn_lb_logits, batch=B, seq_len=S, layer_idx=l,
                                heads=hgrn_heads)
            hd = proj.shape[2]
            head_rows = pl.BlockSpec((hgrn_heads, tm_out, hd), lambda i: (0, i, 0))
            gate_rows = pl.BlockSpec((hgrn_heads, tm_out, hd), lambda i: (4, i, 0))
            x2d, hn = _mix_out_call(
                _hgrn_out_kernel,
                [of, ob, proj, hgrn_out_norm[j].reshape(hgrn_heads, 1, hd)],
                [head_rows, head_rows, gate_rows, _const_spec((hgrn_heads, 1, hd))],
                x2d, hgrn_w_out[j], post_mix_norm[l], pre_ffn_norm[l], tm_out, "hgrn_out")
        last = l == depth - 1
        g_next = pre_mix_norm[l] if last else pre_mix_norm[l + 1]
        outs = _conv_ffn(hn, x2d, S, ffn_w_in[l], ffn_conv_w[l], ffn_conv_b[l], ffn_w_out[l],
                         post_ffn_norm[l], g_next, emit_next=not last)
        if last:
            (x2d,) = outs
        else:
            x2d, hn = outs
    return x2d.reshape(B, S, D)
```

```python
import functools

import jax
import jax.numpy as jnp
import numpy as np
from jax import lax
from jax.experimental import pallas as pl
from jax.experimental.pallas import tpu as pltpu

F32 = jnp.float32
BF16 = jnp.bfloat16

EPS = 1e-6
ROPE_THETA = 10000.0
MLA_NOPE = 128
MLA_ROPE = 64
MLA_V = 128
HGRN_EXPAND = 128
CONV_WIDTH = 3
SCAN_CHUNK = 64
FFN_COL_CHUNK = 256

V7X_LANES = 128
V7X_SUBLANES = 8
V7X_BF16_ROWS = 16
V7X_VMEM_LIMIT = 56 * 1024 * 1024


def _rms(x, gain):
    ms = jnp.mean(x * x, axis=-1, keepdims=True)
    return x * lax.rsqrt(ms + EPS) * gain


def _sigmoid_pair(x):
    e = jnp.exp(-jnp.abs(x))
    r = 1.0 / (1.0 + e)
    er = e * r
    pos = x >= 0
    return jnp.where(pos, r, er), jnp.where(pos, er, r)


def _dot(a, b):
    return jnp.dot(a, b, preferred_element_type=F32)


def _dot_nt(a, b):
    return lax.dot_general(a, b, (((1,), (1,)), ((), ())), preferred_element_type=F32)


def _dot_tn(a, b):
    return lax.dot_general(a, b, (((0,), (0,)), ((), ())), preferred_element_type=F32)


def _params(*sem):
    return pltpu.CompilerParams(dimension_semantics=sem, vmem_limit_bytes=V7X_VMEM_LIMIT)


def _const_spec(shape):
    nd = len(shape)
    return pl.BlockSpec(shape, lambda *_: (0,) * nd)


def _mla_proj_kernel(x_ref, pos_ref, g_pre_ref, w_in_ref, g_q_ref, w_q_ref, g_kv_ref,
                     w_kv_ref, invf_ref, sgn_ref, q_ref, k_ref, v_ref, *, q_lora, kv_lora, heads):
    hn = _rms(x_ref[...], g_pre_ref[...]).astype(BF16)
    proj = _dot(hn, w_in_ref[...])
    c_q = proj[:, :q_lora]
    c_kv = proj[:, q_lora:q_lora + kv_lora]
    kr = proj[:, q_lora + kv_lora:q_lora + kv_lora + MLA_ROPE]
    kr_sw = proj[:, q_lora + kv_lora + MLA_ROPE:]

    ang = pos_ref[...].astype(F32) * invf_ref[...]
    cos2 = jnp.cos(ang)
    sin2 = jnp.sin(ang) * sgn_ref[...]

    q_all = _dot(_rms(c_q, g_q_ref[...]).astype(BF16), w_q_ref[...])
    kv = _dot(_rms(c_kv, g_kv_ref[...]).astype(BF16), w_kv_ref[...])

    n_nope = heads * MLA_NOPE
    n_rope = heads * MLA_ROPE
    k_rot = (kr * cos2[:, :MLA_ROPE] + kr_sw * sin2[:, :MLA_ROPE]).astype(BF16)
    for pair in range(heads // 2):
        lo = n_nope + pair * V7X_LANES
        q_rot = (q_all[:, lo:lo + V7X_LANES] * cos2
                 + q_all[:, lo + n_rope:lo + n_rope + V7X_LANES] * sin2).astype(BF16)
        for sub in range(2):
            h = 2 * pair + sub
            q_ref[h, :, MLA_NOPE:] = q_rot[:, sub * MLA_ROPE:(sub + 1) * MLA_ROPE]
    for h in range(heads):
        q_ref[h, :, :MLA_NOPE] = q_all[:, h * MLA_NOPE:(h + 1) * MLA_NOPE].astype(BF16)
        k_ref[h, :, :MLA_NOPE] = kv[:, h * MLA_NOPE:(h + 1) * MLA_NOPE].astype(BF16)
        k_ref[h, :, MLA_NOPE:] = k_rot
        v_ref[h] = kv[:, n_nope + h * MLA_V:n_nope + (h + 1) * MLA_V].astype(BF16)


def _mla_proj(x, positions, g_pre, w_in, g_q, w_q_up, g_kv, w_kv_up, *, tm=512):
    B, S, D = x.shape
    q_lora = g_q.shape[0]
    kv_lora = g_kv.shape[0]
    heads = w_q_up.shape[1] // (MLA_NOPE + MLA_ROPE)
    half = MLA_ROPE // 2
    qk_dim = MLA_NOPE + MLA_ROPE

    kr_lo = q_lora + kv_lora
    w_in_x = jnp.concatenate(
        [w_in, w_in[:, kr_lo + half:kr_lo + MLA_ROPE], w_in[:, kr_lo:kr_lo + half]], axis=1).astype(BF16)
    wq3 = w_q_up.reshape(q_lora, heads, qk_dim)
    rope = wq3[:, :, MLA_NOPE:]
    rope_sw = jnp.concatenate([rope[:, :, half:], rope[:, :, :half]], axis=-1)
    w_q_x = jnp.concatenate(
        [wq3[:, :, :MLA_NOPE].reshape(q_lora, -1), rope.reshape(q_lora, -1),
         rope_sw.reshape(q_lora, -1)], axis=1).astype(BF16)
    wkv3 = w_kv_up.reshape(kv_lora, heads, MLA_NOPE + MLA_V)
    w_kv_x = jnp.concatenate(
        [wkv3[:, :, :MLA_NOPE].reshape(kv_lora, -1), wkv3[:, :, MLA_NOPE:].reshape(kv_lora, -1)],
        axis=1).astype(BF16)

    inv_freq = 1.0 / (ROPE_THETA ** (jnp.arange(0, MLA_ROPE, 2, dtype=F32) / MLA_ROPE))
    invf = jnp.tile(inv_freq, V7X_LANES // half)[None, :]
    sgn = jnp.tile(jnp.concatenate([-jnp.ones((half,), F32), jnp.ones((half,), F32)]),
                   V7X_LANES // MLA_ROPE)[None, :]
    scale = float(qk_dim) ** -0.5

    kern = functools.partial(_mla_proj_kernel, q_lora=q_lora, kv_lora=kv_lora, heads=heads)
    head_out = lambda width: pl.BlockSpec((None, heads, tm, width), lambda b, i: (b, 0, i, 0))
    return pl.pallas_call(
        kern,
        grid=(B, S // tm),
        in_specs=[
            pl.BlockSpec((None, tm, D), lambda b, i: (b, i, 0)),
            pl.BlockSpec((None, tm, 1), lambda b, i: (b, i, 0)),
            _const_spec((1, D)),
            _const_spec(w_in_x.shape),
            _const_spec((1, q_lora)),
            _const_spec(w_q_x.shape),
            _const_spec((1, kv_lora)),
            _const_spec(w_kv_x.shape),
            _const_spec((1, V7X_LANES)),
            _const_spec((1, V7X_LANES)),
        ],
        out_specs=[head_out(qk_dim), head_out(qk_dim), head_out(MLA_V)],
        out_shape=[
            jax.ShapeDtypeStruct((B, heads, S, qk_dim), BF16),
            jax.ShapeDtypeStruct((B, heads, S, qk_dim), BF16),
            jax.ShapeDtypeStruct((B, heads, S, MLA_V), BF16),
        ],
        compiler_params=_params("parallel", "parallel"),
        name="mla_proj",
    )(x, positions.reshape(B, S, 1), g_pre[None, :], w_in_x, (g_q * scale)[None, :], w_q_x,
      g_kv[None, :], w_kv_x, invf, sgn)


def _attn_kernel(q_ref, k_ref, v_ref, o_ref, *, tq):
    def q_tile(i, carry):
        r0 = pl.multiple_of(i * tq, tq)
        s = _dot_nt(q_ref[pl.ds(r0, tq), :], k_ref[...])
        p = jnp.exp(s - jnp.max(s, axis=-1, keepdims=True))
        denom = jnp.sum(p, axis=-1, keepdims=True)
        o = _dot(p.astype(BF16), v_ref[...])
        o_ref[pl.ds(r0, tq), :] = (o / denom).astype(o_ref.dtype)
        return carry

    lax.fori_loop(0, q_ref.shape[0] // tq, q_tile, 0, unroll=8)


def _attention(q, k, v, *, tq=256):
    B, H, S, dk = q.shape
    dv = v.shape[-1]
    per_head = lambda width: pl.BlockSpec((None, None, S, width), lambda b, h: (b, h, 0, 0))
    return pl.pallas_call(
        functools.partial(_attn_kernel, tq=tq),
        grid=(B, H),
        in_specs=[per_head(dk), per_head(dk), per_head(dv)],
        out_specs=pl.BlockSpec((None, S, dv), lambda b, h: (b, 0, h)),
        out_shape=jax.ShapeDtypeStruct((B, S, H * dv), BF16),
        compiler_params=_params("parallel", "parallel"),
        name="mla_attention",
    )(q, k, v)


def _mix_epilogue(a_bf16, x_ref, w_ref, g_post_ref, g_ffn_ref, x_out_ref, hn_out_ref):
    m = _dot(a_bf16, w_ref[...])
    x1 = x_ref[...] + _rms(m, g_post_ref[...])
    x_out_ref[...] = x1
    hn_out_ref[...] = _rms(x1, g_ffn_ref[...]).astype(BF16)


def _mla_out_kernel(o_ref, x_ref, w_ref, g_post_ref, g_ffn_ref, x_out_ref, hn_out_ref):
    _mix_epilogue(o_ref[...], x_ref, w_ref, g_post_ref, g_ffn_ref, x_out_ref, hn_out_ref)


def _hgrn_out_kernel(of_ref, ob_ref, gate_ref, g_head_ref, x_ref, w_ref, g_post_ref, g_ffn_ref,
                     x_out_ref, hn_out_ref):
    heads = of_ref.shape[0]
    cols = []
    for h in range(heads):
        o = of_ref[h] + ob_ref[h]
        gx = gate_ref[h]
        silu_g = gx * _sigmoid_pair(gx)[0]
        cols.append((_rms(o, g_head_ref[h]) * silu_g).astype(BF16))
    _mix_epilogue(jnp.concatenate(cols, axis=1), x_ref, w_ref, g_post_ref, g_ffn_ref,
                  x_out_ref, hn_out_ref)


def _mix_out_call(kern, lead_args, lead_specs, x2d, w_out, g_post, g_ffn, tm, name):
    T, D = x2d.shape
    row = pl.BlockSpec((tm, D), lambda i: (i, 0))
    return pl.pallas_call(
        kern,
        grid=(T // tm,),
        in_specs=lead_specs + [row, _const_spec(w_out.shape), _const_spec((1, D)), _const_spec((1, D))],
        out_specs=[row, row],
        out_shape=[jax.ShapeDtypeStruct((T, D), F32), jax.ShapeDtypeStruct((T, D), BF16)],
        compiler_params=_params("parallel"),
        name=name,
    )(*lead_args, x2d, w_out.astype(BF16), g_post[None, :], g_ffn[None, :])


def _gelu_tanh(x):
    c = 0.7978845608028654
    return 0.5 * x * (1.0 + jnp.tanh(c * (x + 0.044715 * (x * x * x))))


def _ffn_kernel(hn_ref, hprev_ref, hnext_ref, x_ref, wg_ref, wv_ref, cw_ref, cb_ref, wo_ref,
                g_post_ref, g_next_ref, x_out_ref, *rest, tiles_per_seq, emit_next):
    if emit_next:
        hn_out_ref, act_ref = rest
    else:
        (act_ref,) = rest
    i = pl.program_id(0)
    tm = hn_ref.shape[0]
    hn = hn_ref[...]
    halo = jnp.concatenate([hprev_ref[...], hnext_ref[...]], axis=0)
    t_in_seq = i % tiles_per_seq
    seq_start = t_in_seq == 0
    seq_end = t_in_seq == tiles_per_seq - 1
    rows = lax.broadcasted_iota(jnp.int32, (tm, 1), 0)

    def col_chunk(c, carry):
        wg = wg_ref[c]
        g = _dot(hn, wg)
        gh = _dot(halo, wg)
        val = _dot(hn, wv_ref[c])
        g_before = jnp.where(seq_start, 0.0, gh[V7X_BF16_ROWS - 1:V7X_BF16_ROWS, :])
        g_after = jnp.where(seq_end, 0.0, gh[V7X_BF16_ROWS:V7X_BF16_ROWS + 1, :])
        g_m1 = jnp.where(rows == 0, g_before, pltpu.roll(g, 1, 0))
        g_p1 = jnp.where(rows == tm - 1, g_after, pltpu.roll(g, tm - 1, 0))
        cw = cw_ref[c]
        conv = g_m1 * cw[0:1, :] + g * cw[1:2, :] + g_p1 * cw[2:3, :] + cb_ref[c]
        act_ref[c] = (_gelu_tanh(conv) * val).astype(BF16)
        return carry

    nc = wg_ref.shape[0]
    lax.fori_loop(0, nc, col_chunk, 0, unroll=nc)
    f = _dot(act_ref[0], wo_ref[0])
    for c in range(1, nc):
        f = f + _dot(act_ref[c], wo_ref[c])
    x2 = x_ref[...] + _rms(f, g_post_ref[...])
    x_out_ref[...] = x2
    if emit_next:
        hn_out_ref[...] = _rms(x2, g_next_ref[...]).astype(BF16)


def _conv_ffn(hn, x2d, seq_len, w_in, conv_w, conv_b, w_out, g_post, g_next, *, emit_next, tm=512):
    T, D = x2d.shape
    d_ff = w_out.shape[0]
    cw = FFN_COL_CHUNK
    nc = d_ff // cw
    wg = w_in[:, :d_ff].reshape(D, nc, cw).transpose(1, 0, 2).astype(BF16)
    wv = w_in[:, d_ff:].reshape(D, nc, cw).transpose(1, 0, 2).astype(BF16)
    wo = w_out.reshape(nc, cw, D).astype(BF16)
    cwt = conv_w.reshape(CONV_WIDTH, nc, cw).transpose(1, 0, 2)
    cbt = conv_b.reshape(nc, 1, cw)
    halo_rows = V7X_BF16_ROWS
    per_tile = tm // halo_rows
    n_halo = T // halo_rows
    row = pl.BlockSpec((tm, D), lambda i: (i, 0))
    out_specs = [row]
    out_shape = [jax.ShapeDtypeStruct((T, D), F32)]
    if emit_next:
        out_specs.append(row)
        out_shape.append(jax.ShapeDtypeStruct((T, D), BF16))
    kern = functools.partial(_ffn_kernel, tiles_per_seq=seq_len // tm, emit_next=emit_next)
    return pl.pallas_call(
        kern,
        grid=(T // tm,),
        in_specs=[
            row,
            pl.BlockSpec((halo_rows, D), lambda i: (jnp.maximum(i * per_tile - 1, 0), 0)),
            pl.BlockSpec((halo_rows, D), lambda i: (jnp.minimum((i + 1) * per_tile, n_halo - 1), 0)),
            row,
            _const_spec(wg.shape), _const_spec(wv.shape), _const_spec(cwt.shape),
            _const_spec(cbt.shape), _const_spec(wo.shape),
            _const_spec((1, D)), _const_spec((1, D)),
        ],
        out_specs=out_specs,
        out_shape=out_shape,
        scratch_shapes=[pltpu.VMEM((nc, tm, cw), BF16)],
        compiler_params=_params("parallel"),
        name="conv_ffn",
    )(hn, hn, hn, x2d, wg, wv, cwt, cbt, wo, g_post[None, :], g_next[None, :])


def _hgrn_in_kernel(hn_ref, w_ref, o_ref):
    res = _dot(hn_ref[...], w_ref[...])
    for cb in range(o_ref.shape[0]):
        o_ref[cb] = res[:, cb * V7X_LANES:(cb + 1) * V7X_LANES]


def _hgrn_in_proj(hn, w_in, *, tm=1024, tn=1024):
    T, D = hn.shape
    N = w_in.shape[1]
    return pl.pallas_call(
        _hgrn_in_kernel,
        grid=(N // tn, T // tm),
        in_specs=[pl.BlockSpec((tm, D), lambda n, i: (i, 0)),
                  pl.BlockSpec((D, tn), lambda n, i: (0, n))],
        out_specs=pl.BlockSpec((tn // V7X_LANES, tm, V7X_LANES), lambda n, i: (n, i, 0)),
        out_shape=jax.ShapeDtypeStruct((N // V7X_LANES, T, V7X_LANES), F32),
        compiler_params=_params("parallel", "parallel"),
        name="hgrn_in_proj",
    )(hn, w_in.astype(BF16))


def _scan_one_head(lbl_ref, mask_ref, q_ref, f_ref, v_ref, o_ref, st_ref, d, h, *, rev, layer_idx):
    nb = SCAN_CHUNK // V7X_SUBLANES
    masks = [mask_ref[d, lvl] for lvl in range(mask_ref.shape[1])]
    depth = lbl_ref.shape[1]

    logits = [lbl_ref[d, i, h] for i in range(depth)]
    mx = functools.reduce(jnp.maximum, logits)
    es = [jnp.exp(l - mx) for l in logits]
    tot = functools.reduce(lambda a, b: a + b, es)
    picked = es[1:layer_idx + 1]
    lb = functools.reduce(lambda a, b: a + b, picked) / tot if picked else jnp.zeros_like(tot)
    one_m_lb = 1.0 - lb

    sig_f, sig_nf = _sigmoid_pair(f_ref[h])
    f = lb + one_m_lb * sig_f
    k = one_m_lb * sig_nf
    qx = q_ref[h]
    q = qx * _sigmoid_pair(qx)[0] * (float(HGRN_EXPAND) ** -0.5)
    v = v_ref[h]

    sub = lax.broadcasted_iota(jnp.int32, (V7X_SUBLANES, V7X_LANES), 0)
    il = (V7X_SUBLANES - 1 - sub) if rev else sub

    def blocks(a):
        bl = [a[V7X_SUBLANES * j:V7X_SUBLANES * (j + 1), :] for j in range(nb)]
        return bl[::-1] if rev else bl

    def unblocks(bl):
        return jnp.concatenate(bl[::-1] if rev else bl, axis=0)

    def row(x, pos):
        u = (V7X_SUBLANES - 1 - pos) if rev else pos
        return jnp.broadcast_to(x[u:u + 1, :], x.shape)

    def at_prev(x):
        return pltpu.roll(x, (V7X_SUBLANES - 1) if rev else 1, 0)

    def at_next(x):
        return pltpu.roll(x, 1 if rev else (V7X_SUBLANES - 1), 0)

    qb, kb = blocks(q), blocks(k)
    hi1, hi2, hi4 = [(il & c) != 0 for c in (1, 2, 4)]

    def level_operands(F, G):
        qh = unblocks([a * b for a, b in zip(qb, F)]).astype(BF16)
        kh = unblocks([a * b for a, b in zip(kb, G)] if G is not None else kb).astype(BF16)
        return qh, kh

    F = blocks(f)
    s_acc = None
    qh, kh = level_operands(F, None)
    s_acc = _dot_nt(qh, kh) * masks[0]
    G = [jnp.where(hi1, 1.0, at_next(x)) for x in F]
    F = [x * jnp.where(hi1, at_prev(x), 1.0) for x in F]
    qh, kh = level_operands(F, G)
    s_acc = s_acc + _dot_nt(qh, kh) * masks[1]
    G = [g * jnp.where(hi2, 1.0, jnp.where(hi4, row(x, 7), row(x, 3))) for x, g in zip(F, G)]
    F = [x * jnp.where(hi2, jnp.where(hi4, row(x, 5), row(x, 1)), 1.0) for x in F]
    qh, kh = level_operands(F, G)
    s_acc = s_acc + _dot_nt(qh, kh) * masks[2]
    G = [g * jnp.where(hi4, 1.0, row(x, 7)) for x, g in zip(F, G)]
    F = [x * jnp.where(hi4, row(x, 3), 1.0) for x in F]
    last = V7X_SUBLANES - 1
    for lvl, cb in enumerate((1, 2, 4)):
        qh, kh = level_operands(F, G)
        s_acc = s_acc + _dot_nt(qh, kh) * masks[3 + lvl]
        newF, newG = [], []
        for b in range(nb):
            base = b & ~(2 * cb - 1)
            if b & cb:
                newF.append(F[b] * row(F[base + cb - 1], last))
                newG.append(G[b])
            else:
                newF.append(F[b])
                newG.append(G[b] * row(F[base + 2 * cb - 1], last))
        F, G = newF, newG

    q_dec, k_dec = level_operands(F, G)
    st = st_ref[d, h]
    v16 = v.astype(BF16)
    diag = jnp.sum(q * k, axis=-1, keepdims=True)
    o = _dot_nt(q_dec, st.astype(BF16)) + _dot(s_acc.astype(BF16), v16) + diag * v
    o_ref[h] = o
    f_total = row(F[nb - 1], last)[0:1, :]
    st_ref[d, h] = st * f_total + _dot_tn(v16, k_dec)


def _scan_level_masks():
    n = SCAN_CHUNK
    pos = np.arange(n)
    out = np.zeros((2, n.bit_length() - 1, n, n), np.float32)
    for d in range(2):
        idx = (n - 1 - pos) if d == 1 else pos
        ti, si = idx[:, None], idx[None, :]
        lvl, c = 0, 1
        while c < n:
            out[d, lvl] = ((ti & c) != 0) & ((si & c) == 0) & ((ti // (2 * c)) == (si // (2 * c)))
            lvl, c = lvl + 1, 2 * c
    return out


def _hgrn_scan_kernel(lbl_ref, mask_ref, qf_ref, ff_ref, vf_ref, qb_ref, fb_ref, vb_ref, of_ref,
                      ob_ref, st_ref, *, layer_idx):
    @pl.when(pl.program_id(1) == 0)
    def _():
        st_ref[...] = jnp.zeros_like(st_ref)

    def head(h, carry):
        _scan_one_head(lbl_ref, mask_ref, qf_ref, ff_ref, vf_ref, of_ref, st_ref, 0, h,
                       rev=False, layer_idx=layer_idx)
        _scan_one_head(lbl_ref, mask_ref, qb_ref, fb_ref, vb_ref, ob_ref, st_ref, 1, h,
                       rev=True, layer_idx=layer_idx)
        return carry

    lax.fori_loop(0, qf_ref.shape[0], head, 0, unroll=8)


def _hgrn_scan(proj, lb_logits, *, batch, seq_len, layer_idx, heads):
    T = proj.shape[1]
    hd = proj.shape[2]
    nchunk = seq_len // SCAN_CHUNK
    p5 = proj.reshape(5, heads, batch, seq_len, hd)
    depth = lb_logits.shape[1]
    lbl = lb_logits.reshape(2, depth, heads, 1, hd)
    masks = jnp.asarray(_scan_level_masks())

    def sec(section, backward):
        if backward:
            return pl.BlockSpec((None, heads, None, SCAN_CHUNK, hd),
                                lambda b, j: (section, 0, b, nchunk - 1 - j, 0))
        return pl.BlockSpec((None, heads, None, SCAN_CHUNK, hd), lambda b, j: (section, 0, b, j, 0))

    out_f = pl.BlockSpec((heads, None, SCAN_CHUNK, hd), lambda b, j: (0, b, j, 0))
    out_b = pl.BlockSpec((heads, None, SCAN_CHUNK, hd), lambda b, j: (0, b, nchunk - 1 - j, 0))
    o_sds = jax.ShapeDtypeStruct((heads, batch, seq_len, hd), F32)
    of, ob = pl.pallas_call(
        functools.partial(_hgrn_scan_kernel, layer_idx=layer_idx),
        grid=(batch, nchunk),
        in_specs=[_const_spec(lbl.shape), _const_spec(masks.shape),
                  sec(0, False), sec(1, False), sec(3, False),
                  sec(0, True), sec(2, True), sec(3, True)],
        out_specs=[out_f, out_b],
        out_shape=[o_sds, o_sds],
        scratch_shapes=[pltpu.VMEM((2, heads, hd, hd), F32)],
        compiler_params=_params("arbitrary", "arbitrary"),
        name="hgrn_scan",
    )(lbl, masks, p5, p5, p5, p5, p5, p5)
    return of.reshape(heads, T, hd), ob.reshape(heads, T, hd)


def kernel(x, positions, pre_mix_norm, post_mix_norm, pre_ffn_norm, post_ffn_norm, mla_w_in, mla_q_norm, mla_w_q_up, mla_kv_norm, mla_w_kv_up, mla_w_out, hgrn_w_in, hgrn_lb_logits, hgrn_out_norm, hgrn_w_out, ffn_w_in, ffn_conv_w, ffn_conv_b, ffn_w_out):
    B, S, D = x.shape
    T = B * S
    depth = pre_mix_norm.shape[0]
    n_mixers = 2
    hgrn_heads = D // HGRN_EXPAND
    x2d = x.reshape(T, D)
    hn = None
    tm_out = 512
    for l in range(depth):
        j = l // n_mixers
        if l % n_mixers == 0:
            q, k, v = _mla_proj(x2d.reshape(B, S, D), positions, pre_mix_norm[l], mla_w_in[j],
                                mla_q_norm[j], mla_w_q_up[j], mla_kv_norm[j], mla_w_kv_up[j])
            o = _attention(q, k, v).reshape(T, -1)
            x2d, hn = _mix_out_call(
                _mla_out_kernel, [o], [pl.BlockSpec((tm_out, o.shape[1]), lambda i: (i, 0))],
                x2d, mla_w_out[j], post_mix_norm[l], pre_ffn_norm[l], tm_out, "mla_out")
        else:
            proj = _hgrn_in_proj(hn, hgrn_w_in[j])
            of, ob = _hgrn_scan(proj, hgrn_lb_logits, batch=B, seq_len=S, layer_idx=l,
                                heads=hgrn_heads)
            hd = proj.shape[2]
            head_rows = pl.BlockSpec((hgrn_heads, tm_out, hd), lambda i: (0, i, 0))
            gate_rows = pl.BlockSpec((hgrn_heads, tm_out, hd), lambda i: (4, i, 0))
            x2d, hn = _mix_out_call(
                _hgrn_out_kernel,
                [of, ob, proj, hgrn_out_norm[j].reshape(hgrn_heads, 1, hd)],
                [head_rows, head_rows, gate_rows, _const_spec((hgrn_heads, 1, hd))],
                x2d, hgrn_w_out[j], post_mix_norm[l], pre_ffn_norm[l], tm_out, "hgrn_out")
        last = l == depth - 1
        g_next = pre_mix_norm[l] if last else pre_mix_norm[l + 1]
        outs = _conv_ffn(hn, x2d, S, ffn_w_in[l], ffn_conv_w[l], ffn_conv_b[l], ffn_w_out[l],
                         post_ffn_norm[l], g_next, emit_next=not last)
        if last:
            (x2d,) = outs
        else:
            x2d, hn = outs
    return x2d.reshape(B, S, D)
```

```python
import functools

import jax
import jax.numpy as jnp
import numpy as np
from jax import lax
from jax.experimental import pallas as pl
from jax.experimental.pallas import tpu as pltpu

F32 = jnp.float32
BF16 = jnp.bfloat16

EPS = 1e-6
ROPE_THETA = 10000.0
MLA_NOPE = 128
MLA_ROPE = 64
MLA_V = 128
HGRN_EXPAND = 128
SCAN_CHUNK = 64
SCAN_FAST_MIN_HALF_DECAY = 2.0 ** -100
FFN_COL_CHUNK = 256

V7X_LANES = 128
V7X_SUBLANES = 8
V7X_BF16_ROWS = 16
V7X_VMEM_LIMIT = 56 * 1024 * 1024


def _rms(x, gain):
    ms = jnp.mean(x * x, axis=-1, keepdims=True)
    return x * lax.rsqrt(ms + EPS) * gain


def _sigmoid_pair(x):
    e = jnp.exp(-jnp.abs(x))
    r = 1.0 / (1.0 + e)
    er = e * r
    pos = x >= 0
    return jnp.where(pos, r, er), jnp.where(pos, er, r)


def _dot(a, b):
    return jnp.dot(a, b, preferred_element_type=F32)


def _dot_nt(a, b):
    return lax.dot_general(a, b, (((1,), (1,)), ((), ())), preferred_element_type=F32)


def _dot_tn(a, b):
    return lax.dot_general(a, b, (((0,), (0,)), ((), ())), preferred_element_type=F32)


def _params(*sem):
    return pltpu.CompilerParams(dimension_semantics=sem, vmem_limit_bytes=V7X_VMEM_LIMIT)


def _const_spec(shape):
    nd = len(shape)
    return pl.BlockSpec(shape, lambda *_: (0,) * nd)


def _mla_proj_kernel(x_ref, pos_ref, g_pre_ref, w_in_ref, g_q_ref, w_q_ref, g_kv_ref,
                     w_kv_ref, invf_ref, sgn_ref, q_ref, k_ref, v_ref, *, q_lora, kv_lora, heads):
    hn = _rms(x_ref[...], g_pre_ref[...]).astype(BF16)
    proj = _dot(hn, w_in_ref[...])
    c_q = proj[:, :q_lora]
    c_kv = proj[:, q_lora:q_lora + kv_lora]
    kr = proj[:, q_lora + kv_lora:q_lora + kv_lora + MLA_ROPE]
    kr_sw = proj[:, q_lora + kv_lora + MLA_ROPE:]

    ang = pos_ref[...].astype(F32) * invf_ref[...]
    cos2 = jnp.cos(ang)
    sin2 = jnp.sin(ang) * sgn_ref[...]

    q_all = _dot(_rms(c_q, g_q_ref[...]).astype(BF16), w_q_ref[...])
    kv = _dot(_rms(c_kv, g_kv_ref[...]).astype(BF16), w_kv_ref[...])

    n_nope = heads * MLA_NOPE
    n_rope = heads * MLA_ROPE
    k_rot = (kr * cos2[:, :MLA_ROPE] + kr_sw * sin2[:, :MLA_ROPE]).astype(BF16)
    for pair in range(heads // 2):
        lo = n_nope + pair * V7X_LANES
        q_rot = (q_all[:, lo:lo + V7X_LANES] * cos2
                 + q_all[:, lo + n_rope:lo + n_rope + V7X_LANES] * sin2).astype(BF16)
        for sub in range(2):
            h = 2 * pair + sub
            q_ref[h, :, MLA_NOPE:] = q_rot[:, sub * MLA_ROPE:(sub + 1) * MLA_ROPE]
    for h in range(heads):
        q_ref[h, :, :MLA_NOPE] = q_all[:, h * MLA_NOPE:(h + 1) * MLA_NOPE].astype(BF16)
        k_ref[h, :, :MLA_NOPE] = kv[:, h * MLA_NOPE:(h + 1) * MLA_NOPE].astype(BF16)
        k_ref[h, :, MLA_NOPE:] = k_rot
        v_ref[h] = kv[:, n_nope + h * MLA_V:n_nope + (h + 1) * MLA_V].astype(BF16)


def _mla_proj(x, positions, g_pre, w_in, g_q, w_q_up, g_kv, w_kv_up, *, tm=512):
    B, S, D = x.shape
    q_lora = g_q.shape[0]
    kv_lora = g_kv.shape[0]
    heads = w_q_up.shape[1] // (MLA_NOPE + MLA_ROPE)
    half = MLA_ROPE // 2
    qk_dim = MLA_NOPE + MLA_ROPE

    kr_lo = q_lora + kv_lora
    w_in_x = jnp.concatenate(
        [w_in, w_in[:, kr_lo + half:kr_lo + MLA_ROPE], w_in[:, kr_lo:kr_lo + half]], axis=1).astype(BF16)
    wq3 = w_q_up.reshape(q_lora, heads, qk_dim)
    rope = wq3[:, :, MLA_NOPE:]
    rope_sw = jnp.concatenate([rope[:, :, half:], rope[:, :, :half]], axis=-1)
    w_q_x = jnp.concatenate(
        [wq3[:, :, :MLA_NOPE].reshape(q_lora, -1), rope.reshape(q_lora, -1),
         rope_sw.reshape(q_lora, -1)], axis=1).astype(BF16)
    wkv3 = w_kv_up.reshape(kv_lora, heads, MLA_NOPE + MLA_V)
    w_kv_x = jnp.concatenate(
        [wkv3[:, :, :MLA_NOPE].reshape(kv_lora, -1), wkv3[:, :, MLA_NOPE:].reshape(kv_lora, -1)],
        axis=1).astype(BF16)

    inv_freq = 1.0 / (ROPE_THETA ** (jnp.arange(0, MLA_ROPE, 2, dtype=F32) / MLA_ROPE))
    invf = jnp.tile(inv_freq, V7X_LANES // half)[None, :]
    sgn = jnp.tile(jnp.concatenate([-jnp.ones((half,), F32), jnp.ones((half,), F32)]),
                   V7X_LANES // MLA_ROPE)[None, :]
    scale = float(qk_dim) ** -0.5

    kern = functools.partial(_mla_proj_kernel, q_lora=q_lora, kv_lora=kv_lora, heads=heads)
    head_out = lambda width: pl.BlockSpec((None, heads, tm, width), lambda b, i: (b, 0, i, 0))
    return pl.pallas_call(
        kern,
        grid=(B, S // tm),
        in_specs=[
            pl.BlockSpec((None, tm, D), lambda b, i: (b, i, 0)),
            pl.BlockSpec((None, tm, 1), lambda b, i: (b, i, 0)),
            _const_spec((1, D)),
            _const_spec(w_in_x.shape),
            _const_spec((1, q_lora)),
            _const_spec(w_q_x.shape),
            _const_spec((1, kv_lora)),
            _const_spec(w_kv_x.shape),
            _const_spec((1, V7X_LANES)),
            _const_spec((1, V7X_LANES)),
        ],
        out_specs=[head_out(qk_dim), head_out(qk_dim), head_out(MLA_V)],
        out_shape=[
            jax.ShapeDtypeStruct((B, heads, S, qk_dim), BF16),
            jax.ShapeDtypeStruct((B, heads, S, qk_dim), BF16),
            jax.ShapeDtypeStruct((B, heads, S, MLA_V), BF16),
        ],
        compiler_params=_params("parallel", "parallel"),
        name="mla_proj",
    )(x, positions.reshape(B, S, 1), g_pre[None, :], w_in_x, (g_q * scale)[None, :], w_q_x,
      g_kv[None, :], w_kv_x, invf, sgn)


def _attn_kernel(q_ref, k_ref, v_ref, o_ref, *, tq):
    def q_tile(i, carry):
        r0 = pl.multiple_of(i * tq, tq)
        s = _dot_nt(q_ref[pl.ds(r0, tq), :], k_ref[...])
        p = jnp.exp(s - jnp.max(s, axis=-1, keepdims=True))
        denom = jnp.sum(p, axis=-1, keepdims=True)
        o = _dot(p.astype(BF16), v_ref[...])
        o_ref[pl.ds(r0, tq), :] = (o / denom).astype(o_ref.dtype)
        return carry

    lax.fori_loop(0, q_ref.shape[0] // tq, q_tile, 0, unroll=8)


def _attention(q, k, v, *, tq=256):
    B, H, S, dk = q.shape
    dv = v.shape[-1]
    per_head = lambda width: pl.BlockSpec((None, None, S, width), lambda b, h: (b, h, 0, 0))
    return pl.pallas_call(
        functools.partial(_attn_kernel, tq=tq),
        grid=(B, H),
        in_specs=[per_head(dk), per_head(dk), per_head(dv)],
        out_specs=pl.BlockSpec((None, S, dv), lambda b, h: (b, 0, h)),
        out_shape=jax.ShapeDtypeStruct((B, S, H * dv), BF16),
        compiler_params=_params("parallel", "parallel"),
        name="mla_attention",
    )(q, k, v)


def _mix_epilogue(a_bf16, x_ref, w_ref, g_post_ref, g_ffn_ref, x_out_ref, hn_out_ref):
    m = _dot(a_bf16, w_ref[...])
    x1 = x_ref[...] + _rms(m, g_post_ref[...])
    x_out_ref[...] = x1
    hn_out_ref[...] = _rms(x1, g_ffn_ref[...]).astype(BF16)


def _mla_out_kernel(o_ref, x_ref, w_ref, g_post_ref, g_ffn_ref, x_out_ref, hn_out_ref):
    _mix_epilogue(o_ref[...], x_ref, w_ref, g_post_ref, g_ffn_ref, x_out_ref, hn_out_ref)


def _hgrn_out_kernel(of_ref, ob_ref, gate_ref, g_head_ref, x_ref, w_ref, g_post_ref, g_ffn_ref,
                     x_out_ref, hn_out_ref):
    heads = of_ref.shape[0]
    cols = []
    for h in range(heads):
        o = of_ref[h].astype(F32) + ob_ref[h].astype(F32)
        gx = gate_ref[h]
        silu_g = gx * _sigmoid_pair(gx)[0]
        cols.append((_rms(o, g_head_ref[h]) * silu_g).astype(BF16))
    _mix_epilogue(jnp.concatenate(cols, axis=1), x_ref, w_ref, g_post_ref, g_ffn_ref,
                  x_out_ref, hn_out_ref)


def _mix_out_call(kern, lead_args, lead_specs, x2d, w_out, g_post, g_ffn, tm, name):
    T, D = x2d.shape
    row = pl.BlockSpec((tm, D), lambda i: (i, 0))
    return pl.pallas_call(
        kern,
        grid=(T // tm,),
        in_specs=lead_specs + [row, _const_spec(w_out.shape), _const_spec((1, D)), _const_spec((1, D))],
        out_specs=[row, row],
        out_shape=[jax.ShapeDtypeStruct((T, D), F32), jax.ShapeDtypeStruct((T, D), BF16)],
        compiler_params=_params("parallel"),
        name=name,
    )(*lead_args, x2d, w_out.astype(BF16), g_post[None, :], g_ffn[None, :])


def _gelu_tanh(x):
    c = 0.7978845608028654
    return 0.5 * x * (1.0 + jnp.tanh(c * (x + 0.044715 * (x * x * x))))


def _ffn_kernel(hn_ref, hprev_ref, hnext_ref, x_ref, w_in_ref, cw_ref, cb_ref, wo_ref,
                g_post_ref, g_next_ref, x_out_ref, *rest, tiles_per_seq, emit_next):
    if emit_next:
        hn_out_ref, act_ref = rest
    else:
        (act_ref,) = rest
    i = pl.program_id(0)
    tm = hn_ref.shape[0]
    hn = hn_ref[...]
    halo = jnp.concatenate([hprev_ref[...], hnext_ref[...]], axis=0)
    t_in_seq = i % tiles_per_seq
    seq_start = t_in_seq == 0
    seq_end = t_in_seq == tiles_per_seq - 1
    rows = lax.broadcasted_iota(jnp.int32, (tm, 1), 0)

    nc, _, cw_cols = act_ref.shape
    d_ff = nc * cw_cols

    def col_chunk(c):
        cols = slice(c * cw_cols, (c + 1) * cw_cols)
        wg = w_in_ref[:, cols]
        g = _dot(hn, wg)
        gh = _dot(halo, wg)
        val = _dot(hn, w_in_ref[:, d_ff + c * cw_cols:d_ff + (c + 1) * cw_cols])
        g_before = jnp.where(seq_start, 0.0, gh[V7X_BF16_ROWS - 1:V7X_BF16_ROWS, :])
        g_after = jnp.where(seq_end, 0.0, gh[V7X_BF16_ROWS:V7X_BF16_ROWS + 1, :])
        g_m1 = jnp.where(rows == 0, g_before, pltpu.roll(g, 1, 0))
        g_p1 = jnp.where(rows == tm - 1, g_after, pltpu.roll(g, tm - 1, 0))
        cw = cw_ref[:, cols]
        conv = g_m1 * cw[0:1, :] + g * cw[1:2, :] + g_p1 * cw[2:3, :] + cb_ref[:, cols]
        act_ref[c] = (_gelu_tanh(conv) * val).astype(BF16)

    for c in range(nc):
        col_chunk(c)
    f = _dot(act_ref[0], wo_ref[:cw_cols, :])
    for c in range(1, nc):
        f = f + _dot(act_ref[c], wo_ref[c * cw_cols:(c + 1) * cw_cols, :])
    x2 = x_ref[...] + _rms(f, g_post_ref[...])
    x_out_ref[...] = x2
    if emit_next:
        hn_out_ref[...] = _rms(x2, g_next_ref[...]).astype(BF16)


def _conv_ffn(hn, x2d, seq_len, w_in, conv_w, conv_b, w_out, g_post, g_next, *, emit_next, tm=512):
    T, D = x2d.shape
    d_ff = w_out.shape[0]
    cw = FFN_COL_CHUNK
    nc = d_ff // cw
    w_in16 = w_in.astype(BF16)
    wo = w_out.astype(BF16)
    cbt = conv_b[None, :]
    halo_rows = V7X_BF16_ROWS
    per_tile = tm // halo_rows
    n_halo = T // halo_rows
    row = pl.BlockSpec((tm, D), lambda i: (i, 0))
    out_specs = [row]
    out_shape = [jax.ShapeDtypeStruct((T, D), F32)]
    if emit_next:
        out_specs.append(row)
        out_shape.append(jax.ShapeDtypeStruct((T, D), BF16))
    kern = functools.partial(_ffn_kernel, tiles_per_seq=seq_len // tm, emit_next=emit_next)
    return pl.pallas_call(
        kern,
        grid=(T // tm,),
        in_specs=[
            row,
            pl.BlockSpec((halo_rows, D), lambda i: (jnp.maximum(i * per_tile - 1, 0), 0)),
            pl.BlockSpec((halo_rows, D), lambda i: (jnp.minimum((i + 1) * per_tile, n_halo - 1), 0)),
            row,
            _const_spec(w_in16.shape), _const_spec(conv_w.shape),
            _const_spec(cbt.shape), _const_spec(wo.shape),
            _const_spec((1, D)), _const_spec((1, D)),
        ],
        out_specs=out_specs,
        out_shape=out_shape,
        scratch_shapes=[pltpu.VMEM((nc, tm, cw), BF16)],
        compiler_params=_params("parallel"),
        name="conv_ffn",
    )(hn, hn, hn, x2d, w_in16, conv_w, cbt, wo, g_post[None, :], g_next[None, :])


def _hgrn_in_kernel(hn_ref, w_ref, o_ref):
    res = _dot(hn_ref[...], w_ref[...])
    for cb in range(o_ref.shape[0]):
        o_ref[cb] = res[:, cb * V7X_LANES:(cb + 1) * V7X_LANES]


def _hgrn_in_proj(hn, w_in, *, tm=1024, tn=1024):
    T, D = hn.shape
    N = w_in.shape[1]
    return pl.pallas_call(
        _hgrn_in_kernel,
        grid=(N // tn, T // tm),
        in_specs=[pl.BlockSpec((tm, D), lambda n, i: (i, 0)),
                  pl.BlockSpec((D, tn), lambda n, i: (0, n))],
        out_specs=pl.BlockSpec((tn // V7X_LANES, tm, V7X_LANES), lambda n, i: (n, i, 0)),
        out_shape=jax.ShapeDtypeStruct((N // V7X_LANES, T, V7X_LANES), F32),
        compiler_params=_params("parallel", "parallel"),
        name="hgrn_in_proj",
    )(hn, w_in.astype(BF16))


def _scan_gates(lbl_ref, q_ref, f_ref, d, h, layer_idx):
    depth = lbl_ref.shape[1]
    logits = [lbl_ref[d, i, h] for i in range(depth)]
    mx = functools.reduce(jnp.maximum, logits)
    es = [jnp.exp(l - mx) for l in logits]
    tot = functools.reduce(lambda a, b: a + b, es)
    picked = es[1:layer_idx + 1]
    lb = functools.reduce(lambda a, b: a + b, picked) / tot if picked else jnp.zeros_like(tot)
    one_m_lb = 1.0 - lb
    sig_f, sig_nf = _sigmoid_pair(f_ref[h])
    qx = q_ref[h]
    q = qx * _sigmoid_pair(qx)[0] * (float(HGRN_EXPAND) ** -0.5)
    return q, lb + one_m_lb * sig_f, one_m_lb * sig_nf


def _half_chunk_decay(f):
    nb = SCAN_CHUNK // V7X_SUBLANES
    out = []
    for half in range(2):
        bl = [f[V7X_SUBLANES * j:V7X_SUBLANES * (j + 1), :] for j in range(half * nb // 2, (half + 1) * nb // 2)]
        p = functools.reduce(lambda a, b: a * b, bl)
        for shift in (4, 2, 1):
            p = p * pltpu.roll(p, shift, 0)
        out.append(p)
    return out


def _scan_core(q, f, k, v, mask_ref, o_ref, st_ref, d, h, *, rev, fast):
    nb = SCAN_CHUNK // V7X_SUBLANES
    n_levels = mask_ref.shape[1] - 1

    sub = lax.broadcasted_iota(jnp.int32, (V7X_SUBLANES, V7X_LANES), 0)
    il = (V7X_SUBLANES - 1 - sub) if rev else sub

    def blocks(a):
        bl = [a[V7X_SUBLANES * j:V7X_SUBLANES * (j + 1), :] for j in range(nb)]
        return bl[::-1] if rev else bl

    def unblocks(bl):
        return jnp.concatenate(bl[::-1] if rev else bl, axis=0)

    def row(x, pos):
        u = (V7X_SUBLANES - 1 - pos) if rev else pos
        return jnp.broadcast_to(x[u:u + 1, :], x.shape)

    def at_prev(x):
        return pltpu.roll(x, (V7X_SUBLANES - 1) if rev else 1, 0)

    def at_next(x):
        return pltpu.roll(x, 1 if rev else (V7X_SUBLANES - 1), 0)

    qb, kb = blocks(q), blocks(k)
    hi1, hi2, hi4 = [(il & c) != 0 for c in (1, 2, 4)]

    def level_operands(F, G):
        qh = unblocks([a * b for a, b in zip(qb, F)]).astype(BF16)
        kh = unblocks([a * b for a, b in zip(kb, G)] if G is not None else kb).astype(BF16)
        return qh, kh

    last = V7X_SUBLANES - 1

    def next_level(F, G, lvl):
        if lvl == 0:
            return ([x * jnp.where(hi1, at_prev(x), 1.0) for x in F],
                    [jnp.where(hi1, 1.0, at_next(x)) for x in F])
        if lvl == 1:
            return ([x * jnp.where(hi2, jnp.where(hi4, row(x, 5), row(x, 1)), 1.0) for x in F],
                    [g * jnp.where(hi2, 1.0, jnp.where(hi4, row(x, 7), row(x, 3))) for x, g in zip(F, G)])
        if lvl == 2:
            return ([x * jnp.where(hi4, row(x, 3), 1.0) for x in F],
                    [g * jnp.where(hi4, 1.0, row(x, 7)) for x, g in zip(F, G)])
        cb = 2 ** (lvl - 3)
        newF, newG = [], []
        for b in range(nb):
            base = b & ~(2 * cb - 1)
            if b & cb:
                newF.append(F[b] * row(F[base + cb - 1], last))
                newG.append(G[b])
            else:
                newF.append(F[b])
                newG.append(G[b] * row(F[base + 2 * cb - 1], last))
        return newF, newG

    F, G = blocks(f), None
    if fast:
        for lvl in range(n_levels - 1):
            F, G = next_level(F, G, lvl)
        half = nb // 2
        q_fac = [1.0 / G[b] if b < half else F[b] for b in range(nb)]
        k_fac = [G[b] if b < half else 1.0 / F[b] for b in range(nb)]
        qh = unblocks([a * b for a, b in zip(qb, q_fac)]).astype(BF16)
        kh = unblocks([a * b for a, b in zip(kb, k_fac)]).astype(BF16)
        s_acc = jnp.where(mask_ref[d, n_levels] != 0.0, _dot_nt(qh, kh), 0.0)
        F, G = next_level(F, G, n_levels - 1)
    else:
        s_acc = None
        for lvl in range(n_levels):
            qh, kh = level_operands(F, G)
            term = _dot_nt(qh, kh) * mask_ref[d, lvl]
            s_acc = term if s_acc is None else s_acc + term
            F, G = next_level(F, G, lvl)

    q_dec, k_dec = level_operands(F, G)
    st = st_ref[d, h]
    v16 = v.astype(BF16)
    diag = jnp.sum(q * k, axis=-1, keepdims=True)
    o = _dot_nt(q_dec, st.astype(BF16)) + _dot(s_acc.astype(BF16), v16) + diag * v
    o_ref[h] = o.astype(o_ref.dtype)
    f_total = row(F[nb - 1], last)[0:1, :]
    st_ref[d, h] = st * f_total + _dot_tn(v16, k_dec)


def _scan_level_masks():
    n = SCAN_CHUNK
    pos = np.arange(n)
    n_levels = n.bit_length() - 1
    out = np.zeros((2, n_levels + 1, n, n), np.float32)
    for d in range(2):
        idx = (n - 1 - pos) if d == 1 else pos
        ti, si = idx[:, None], idx[None, :]
        lvl, c = 0, 1
        while c < n:
            out[d, lvl] = ((ti & c) != 0) & ((si & c) == 0) & ((ti // (2 * c)) == (si // (2 * c)))
            lvl, c = lvl + 1, 2 * c
        out[d, n_levels] = out[d, :n_levels].sum(axis=0)
    return out


def _hgrn_scan_kernel(lbl_ref, mask_ref, qf_ref, ff_ref, vf_ref, qb_ref, fb_ref, vb_ref, of_ref,
                      ob_ref, st_ref, gate_ref, *, layer_idx):
    @pl.when(pl.program_id(1) == 0)
    def _():
        st_ref[...] = jnp.zeros_like(st_ref)

    heads = qf_ref.shape[0]
    dirs = ((qf_ref, ff_ref, vf_ref, of_ref), (qb_ref, fb_ref, vb_ref, ob_ref))

    def gate_head(h, weakest):
        for d, (q_ref, f_ref, _, _) in enumerate(dirs):
            q, f, k = _scan_gates(lbl_ref, q_ref, f_ref, d, h, layer_idx)
            gate_ref[0, d, h] = q
            gate_ref[1, d, h] = f
            gate_ref[2, d, h] = k
            for p in _half_chunk_decay(f):
                weakest = jnp.minimum(weakest, p)
        return weakest

    weakest = lax.fori_loop(0, heads, gate_head,
                            jnp.ones((V7X_SUBLANES, V7X_LANES), F32), unroll=heads)
    fast_ok = jnp.min(weakest) >= SCAN_FAST_MIN_HALF_DECAY

    def one(h, d, fast):
        _, _, v_ref, o_ref = dirs[d]
        _scan_core(gate_ref[0, d, h], gate_ref[1, d, h], gate_ref[2, d, h], v_ref[h], mask_ref,
                   o_ref, st_ref, d, h, rev=d == 1, fast=fast)

    @pl.when(fast_ok)
    def _():
        for h in range(heads):
            for d in range(2):
                one(h, d, True)

    @pl.when(jnp.logical_not(fast_ok))
    def _():
        def head(h, carry):
            for d in range(2):
                one(h, d, False)
            return carry
        lax.fori_loop(0, heads, head, 0, unroll=2)


def _hgrn_scan(proj, lb_logits, *, batch, seq_len, layer_idx, heads):
    T = proj.shape[1]
    hd = proj.shape[2]
    nchunk = seq_len // SCAN_CHUNK
    p5 = proj.reshape(5, heads, batch, seq_len, hd)
    depth = lb_logits.shape[1]
    lbl = lb_logits.reshape(2, depth, heads, 1, hd)
    masks = jnp.asarray(_scan_level_masks())

    def sec(section, backward):
        if backward:
            return pl.BlockSpec((None, heads, None, SCAN_CHUNK, hd),
                                lambda b, j: (section, 0, b, nchunk - 1 - j, 0))
        return pl.BlockSpec((None, heads, None, SCAN_CHUNK, hd), lambda b, j: (section, 0, b, j, 0))

    out_f = pl.BlockSpec((heads, None, SCAN_CHUNK, hd), lambda b, j: (0, b, j, 0))
    out_b = pl.BlockSpec((heads, None, SCAN_CHUNK, hd), lambda b, j: (0, b, nchunk - 1 - j, 0))
    o_sds = jax.ShapeDtypeStruct((heads, batch, seq_len, hd), BF16)
    of, ob = pl.pallas_call(
        functools.partial(_hgrn_scan_kernel, layer_idx=layer_idx),
        grid=(batch, nchunk),
        in_specs=[_const_spec(lbl.shape), _const_spec(masks.shape),
                  sec(0, False), sec(1, False), sec(3, False),
                  sec(0, True), sec(2, True), sec(3, True)],
        out_specs=[out_f, out_b],
        out_shape=[o_sds, o_sds],
        scratch_shapes=[pltpu.VMEM((2, heads, hd, hd), F32),
                        pltpu.VMEM((3, 2, heads, SCAN_CHUNK, hd), F32)],
        compiler_params=_params("arbitrary", "arbitrary"),
        name="hgrn_scan",
    )(lbl, masks, p5, p5, p5, p5, p5, p5)
    return of.reshape(heads, T, hd), ob.reshape(heads, T, hd)


def kernel(x, positions, pre_mix_norm, post_mix_norm, pre_ffn_norm, post_ffn_norm, mla_w_in, mla_q_norm, mla_w_q_up, mla_kv_norm, mla_w_kv_up, mla_w_out, hgrn_w_in, hgrn_lb_logits, hgrn_out_norm, hgrn_w_out, ffn_w_in, ffn_conv_w, ffn_conv_b, ffn_w_out):
    B, S, D = x.shape
    T = B * S
    depth = pre_mix_norm.shape[0]
    n_mixers = 2
    hgrn_heads = D // HGRN_EXPAND
    x2d = x.reshape(T, D)
    hn = None
    tm_out = 512
    for l in range(depth):
        j = l // n_mixers
        if l % n_mixers == 0:
            q, k, v = _mla_proj(x2d.reshape(B, S, D), positions, pre_mix_norm[l], mla_w_in[j],
                                mla_q_norm[j], mla_w_q_up[j], mla_kv_norm[j], mla_w_kv_up[j])
            o = _attention(q, k, v).reshape(T, -1)
            x2d, hn = _mix_out_call(
                _mla_out_kernel, [o], [pl.BlockSpec((tm_out, o.shape[1]), lambda i: (i, 0))],
                x2d, mla_w_out[j], post_mix_norm[l], pre_ffn_norm[l], tm_out, "mla_out")
        else:
            proj = _hgrn_in_proj(hn, hgrn_w_in[j])
            of, ob = _hgrn_scan(proj, hgrn_lb_logits, batch=B, seq_len=S, layer_idx=l,
                                heads=hgrn_heads)
            hd = proj.shape[2]
            head_rows = pl.BlockSpec((hgrn_heads, tm_out, hd), lambda i: (0, i, 0))
            gate_rows = pl.BlockSpec((hgrn_heads, tm_out, hd), lambda i: (4, i, 0))
            x2d, hn = _mix_out_call(
                _hgrn_out_kernel,
                [of, ob, proj, hgrn_out_norm[j].reshape(hgrn_heads, 1, hd)],
                [head_rows, head_rows, gate_rows, _const_spec((hgrn_heads, 1, hd))],
                x2d, hgrn_w_out[j], post_mix_norm[l], pre_ffn_norm[l], tm_out, "hgrn_out")
        last = l == depth - 1
        g_next = pre_mix_norm[l] if last else pre_mix_norm[l + 1]
        outs = _conv_ffn(hn, x2d, S, ffn_w_in[l], ffn_conv_w[l], ffn_conv_b[l], ffn_w_out[l],
                         post_ffn_norm[l], g_next, emit_next=not last)
        if last:
            (x2d,) = outs
        else:
            x2d, hn = outs
    return x2d.reshape(B, S, D)
```

```python
import functools

import jax
import jax.numpy as jnp
import numpy as np
from jax import lax
from jax.experimental import pallas as pl
from jax.experimental.pallas import tpu as pltpu

F32 = jnp.float32
BF16 = jnp.bfloat16

EPS = 1e-6
LOG2_E = 1.4426950408889634
ROPE_THETA = 10000.0
MLA_NOPE = 128
MLA_ROPE = 64
MLA_V = 128
HGRN_EXPAND = 128
SCAN_CHUNK = 64
SCAN_FAST_MIN_HALF_DECAY = 2.0 ** -100
FFN_COL_CHUNK = 256

V7X_LANES = 128
V7X_SUBLANES = 8
V7X_BF16_ROWS = 16
V7X_VMEM_LIMIT = 56 * 1024 * 1024


def _rms(x, gain):
    ms = jnp.mean(x * x, axis=-1, keepdims=True)
    return x * lax.rsqrt(ms + EPS) * gain


def _sigmoid_pair(x):
    e = jnp.exp(-jnp.abs(x))
    r = 1.0 / (1.0 + e)
    er = e * r
    pos = x >= 0
    return jnp.where(pos, r, er), jnp.where(pos, er, r)


def _dot(a, b):
    return jnp.dot(a, b, preferred_element_type=F32)


def _dot_nt(a, b):
    return lax.dot_general(a, b, (((1,), (1,)), ((), ())), preferred_element_type=F32)


def _dot_tn(a, b):
    return lax.dot_general(a, b, (((0,), (0,)), ((), ())), preferred_element_type=F32)


def _params(*sem):
    return pltpu.CompilerParams(dimension_semantics=sem, vmem_limit_bytes=V7X_VMEM_LIMIT)


def _const_spec(shape):
    nd = len(shape)
    return pl.BlockSpec(shape, lambda *_: (0,) * nd)


def _mla_proj_kernel(x_ref, pos_ref, g_pre_ref, w_in_ref, g_q_ref, w_q_ref, g_kv_ref,
                     w_kv_ref, invf_ref, sgn_ref, q_ref, k_ref, v_ref, *, q_lora, kv_lora, heads):
    hn = _rms(x_ref[...], g_pre_ref[...]).astype(BF16)
    proj = _dot(hn, w_in_ref[...])
    c_q = proj[:, :q_lora]
    c_kv = proj[:, q_lora:q_lora + kv_lora]
    kr = proj[:, q_lora + kv_lora:q_lora + kv_lora + MLA_ROPE]
    kr_sw = proj[:, q_lora + kv_lora + MLA_ROPE:]

    ang = pos_ref[...].astype(F32) * invf_ref[...]
    cos2 = jnp.cos(ang)
    sin2 = jnp.sin(ang) * sgn_ref[...]

    q_all = _dot(_rms(c_q, g_q_ref[...]).astype(BF16), w_q_ref[...])
    kv = _dot(_rms(c_kv, g_kv_ref[...]).astype(BF16), w_kv_ref[...])

    n_nope = heads * MLA_NOPE
    n_rope = heads * MLA_ROPE
    k_rot = (kr * cos2[:, :MLA_ROPE] + kr_sw * sin2[:, :MLA_ROPE]).astype(BF16)
    for pair in range(heads // 2):
        lo = n_nope + pair * V7X_LANES
        q_rot = (q_all[:, lo:lo + V7X_LANES] * cos2
                 + q_all[:, lo + n_rope:lo + n_rope + V7X_LANES] * sin2).astype(BF16)
        for sub in range(2):
            h = 2 * pair + sub
            q_ref[h, :, MLA_NOPE:] = q_rot[:, sub * MLA_ROPE:(sub + 1) * MLA_ROPE]
    for h in range(heads):
        q_ref[h, :, :MLA_NOPE] = q_all[:, h * MLA_NOPE:(h + 1) * MLA_NOPE].astype(BF16)
        k_ref[h, :, :MLA_NOPE] = kv[:, h * MLA_NOPE:(h + 1) * MLA_NOPE].astype(BF16)
        k_ref[h, :, MLA_NOPE:] = k_rot
        v_ref[h, :, :MLA_V] = kv[:, n_nope + h * MLA_V:n_nope + (h + 1) * MLA_V].astype(BF16)
        v_ref[h, :, MLA_V:] = jnp.ones((v_ref.shape[1], MLA_V), BF16)


def _mla_proj(x, positions, g_pre, w_in, g_q, w_q_up, g_kv, w_kv_up, *, tm=512):
    B, S, D = x.shape
    q_lora = g_q.shape[0]
    kv_lora = g_kv.shape[0]
    heads = w_q_up.shape[1] // (MLA_NOPE + MLA_ROPE)
    half = MLA_ROPE // 2
    qk_dim = MLA_NOPE + MLA_ROPE

    kr_lo = q_lora + kv_lora
    w_in_x = jnp.concatenate(
        [w_in, w_in[:, kr_lo + half:kr_lo + MLA_ROPE], w_in[:, kr_lo:kr_lo + half]], axis=1).astype(BF16)
    wq3 = w_q_up.reshape(q_lora, heads, qk_dim)
    rope = wq3[:, :, MLA_NOPE:]
    rope_sw = jnp.concatenate([rope[:, :, half:], rope[:, :, :half]], axis=-1)
    w_q_x = jnp.concatenate(
        [wq3[:, :, :MLA_NOPE].reshape(q_lora, -1), rope.reshape(q_lora, -1),
         rope_sw.reshape(q_lora, -1)], axis=1).astype(BF16)
    wkv3 = w_kv_up.reshape(kv_lora, heads, MLA_NOPE + MLA_V)
    w_kv_x = jnp.concatenate(
        [wkv3[:, :, :MLA_NOPE].reshape(kv_lora, -1), wkv3[:, :, MLA_NOPE:].reshape(kv_lora, -1)],
        axis=1).astype(BF16)

    inv_freq = 1.0 / (ROPE_THETA ** (jnp.arange(0, MLA_ROPE, 2, dtype=F32) / MLA_ROPE))
    invf = jnp.tile(inv_freq, V7X_LANES // half)[None, :]
    sgn = jnp.tile(jnp.concatenate([-jnp.ones((half,), F32), jnp.ones((half,), F32)]),
                   V7X_LANES // MLA_ROPE)[None, :]
    scale = float(qk_dim) ** -0.5 * LOG2_E

    kern = functools.partial(_mla_proj_kernel, q_lora=q_lora, kv_lora=kv_lora, heads=heads)
    head_out = lambda width: pl.BlockSpec((None, heads, tm, width), lambda b, i: (b, 0, i, 0))
    return pl.pallas_call(
        kern,
        grid=(B, S // tm),
        in_specs=[
            pl.BlockSpec((None, tm, D), lambda b, i: (b, i, 0)),
            pl.BlockSpec((None, tm, 1), lambda b, i: (b, i, 0)),
            _const_spec((1, D)),
            _const_spec(w_in_x.shape),
            _const_spec((1, q_lora)),
            _const_spec(w_q_x.shape),
            _const_spec((1, kv_lora)),
            _const_spec(w_kv_x.shape),
            _const_spec((1, V7X_LANES)),
            _const_spec((1, V7X_LANES)),
        ],
        out_specs=[head_out(qk_dim), head_out(qk_dim), head_out(2 * MLA_V)],
        out_shape=[
            jax.ShapeDtypeStruct((B, heads, S, qk_dim), BF16),
            jax.ShapeDtypeStruct((B, heads, S, qk_dim), BF16),
            jax.ShapeDtypeStruct((B, heads, S, 2 * MLA_V), BF16),
        ],
        compiler_params=_params("parallel", "parallel"),
        name="mla_proj",
    )(x, positions.reshape(B, S, 1), g_pre[None, :], w_in_x, (g_q * scale)[None, :], w_q_x,
      g_kv[None, :], w_kv_x, invf, sgn)


def _attn_kernel(q_ref, k_ref, v_ref, o_ref, *, tq):
    def q_tile(i, carry):
        r0 = pl.multiple_of(i * tq, tq)
        s = _dot_nt(q_ref[pl.ds(r0, tq), :], k_ref[...])
        p = jnp.exp2(s - jnp.max(s, axis=-1, keepdims=True))
        o = _dot(p.astype(BF16), v_ref[...])
        dv = o_ref.shape[1]
        o_ref[pl.ds(r0, tq), :] = (o[:, :dv] / o[:, dv:]).astype(o_ref.dtype)
        return carry

    lax.fori_loop(0, q_ref.shape[0] // tq, q_tile, 0, unroll=8)


def _attention(q, k, v, *, tq=256):
    B, H, S, dk = q.shape
    dv = v.shape[-1] // 2
    per_head = lambda width: pl.BlockSpec((None, None, S, width), lambda b, h: (b, h, 0, 0))
    return pl.pallas_call(
        functools.partial(_attn_kernel, tq=tq),
        grid=(B, H),
        in_specs=[per_head(dk), per_head(dk), per_head(2 * dv)],
        out_specs=pl.BlockSpec((None, S, dv), lambda b, h: (b, 0, h)),
        out_shape=jax.ShapeDtypeStruct((B, S, H * dv), BF16),
        compiler_params=_params("parallel", "parallel"),
        name="mla_attention",
    )(q, k, v)


def _mix_epilogue(a_bf16, x_ref, w_ref, g_post_ref, g_ffn_ref, x_out_ref, hn_out_ref):
    m = _dot(a_bf16, w_ref[...])
    x1 = x_ref[...] + _rms(m, g_post_ref[...])
    x_out_ref[...] = x1
    hn_out_ref[...] = _rms(x1, g_ffn_ref[...]).astype(BF16)


def _mla_out_kernel(o_ref, x_ref, w_ref, g_post_ref, g_ffn_ref, x_out_ref, hn_out_ref):
    _mix_epilogue(o_ref[...], x_ref, w_ref, g_post_ref, g_ffn_ref, x_out_ref, hn_out_ref)


def _hgrn_out_kernel(of_ref, ob_ref, gate_ref, g_head_ref, x_ref, w_ref, g_post_ref, g_ffn_ref,
                     x_out_ref, hn_out_ref):
    heads = of_ref.shape[0]
    cols = []
    for h in range(heads):
        o = of_ref[h].astype(F32) + ob_ref[h].astype(F32)
        gx = gate_ref[h]
        silu_g = gx * _sigmoid_pair(gx)[0]
        cols.append((_rms(o, g_head_ref[h]) * silu_g).astype(BF16))
    _mix_epilogue(jnp.concatenate(cols, axis=1), x_ref, w_ref, g_post_ref, g_ffn_ref,
                  x_out_ref, hn_out_ref)


def _mix_out_call(kern, lead_args, lead_specs, x2d, w_out, g_post, g_ffn, tm, name):
    T, D = x2d.shape
    row = pl.BlockSpec((tm, D), lambda i: (i, 0))
    return pl.pallas_call(
        kern,
        grid=(T // tm,),
        in_specs=lead_specs + [row, _const_spec(w_out.shape), _const_spec((1, D)), _const_spec((1, D))],
        out_specs=[row, row],
        out_shape=[jax.ShapeDtypeStruct((T, D), F32), jax.ShapeDtypeStruct((T, D), BF16)],
        compiler_params=_params("parallel"),
        name=name,
    )(*lead_args, x2d, w_out.astype(BF16), g_post[None, :], g_ffn[None, :])


def _gelu_tanh(x):
    c = 0.7978845608028654
    return 0.5 * x * (1.0 + jnp.tanh(c * (x + 0.044715 * (x * x * x))))


def _ffn_kernel(hn_ref, hprev_ref, hnext_ref, x_ref, w_in_ref, cw_ref, cb_ref, wo_ref,
                g_post_ref, g_next_ref, x_out_ref, *rest, tiles_per_seq, emit_next):
    if emit_next:
        hn_out_ref, act_ref = rest
    else:
        (act_ref,) = rest
    i = pl.program_id(0)
    tm = hn_ref.shape[0]
    hn = hn_ref[...]
    halo = jnp.concatenate([hprev_ref[...], hnext_ref[...]], axis=0)
    t_in_seq = i % tiles_per_seq
    seq_start = t_in_seq == 0
    seq_end = t_in_seq == tiles_per_seq - 1
    rows = lax.broadcasted_iota(jnp.int32, (tm, 1), 0)

    nc, _, cw_cols = act_ref.shape
    d_ff = nc * cw_cols

    def col_chunk(c):
        cols = slice(c * cw_cols, (c + 1) * cw_cols)
        wg = w_in_ref[:, cols]
        g = _dot(hn, wg)
        gh = _dot(halo, wg)
        val = _dot(hn, w_in_ref[:, d_ff + c * cw_cols:d_ff + (c + 1) * cw_cols])
        g_before = jnp.where(seq_start, 0.0, gh[V7X_BF16_ROWS - 1:V7X_BF16_ROWS, :])
        g_after = jnp.where(seq_end, 0.0, gh[V7X_BF16_ROWS:V7X_BF16_ROWS + 1, :])
        g_m1 = jnp.where(rows == 0, g_before, pltpu.roll(g, 1, 0))
        g_p1 = jnp.where(rows == tm - 1, g_after, pltpu.roll(g, tm - 1, 0))
        cw = cw_ref[:, cols]
        conv = g_m1 * cw[0:1, :] + g * cw[1:2, :] + g_p1 * cw[2:3, :] + cb_ref[:, cols]
        act_ref[c] = (_gelu_tanh(conv) * val).astype(BF16)

    for c in range(nc):
        col_chunk(c)
    f = _dot(act_ref[0], wo_ref[:cw_cols, :])
    for c in range(1, nc):
        f = f + _dot(act_ref[c], wo_ref[c * cw_cols:(c + 1) * cw_cols, :])
    x2 = x_ref[...] + _rms(f, g_post_ref[...])
    x_out_ref[...] = x2
    if emit_next:
        hn_out_ref[...] = _rms(x2, g_next_ref[...]).astype(BF16)


def _conv_ffn(hn, x2d, seq_len, w_in, conv_w, conv_b, w_out, g_post, g_next, *, emit_next, tm=512):
    T, D = x2d.shape
    d_ff = w_out.shape[0]
    cw = FFN_COL_CHUNK
    nc = d_ff // cw
    w_in16 = w_in.astype(BF16)
    wo = w_out.astype(BF16)
    cbt = conv_b[None, :]
    halo_rows = V7X_BF16_ROWS
    per_tile = tm // halo_rows
    n_halo = T // halo_rows
    row = pl.BlockSpec((tm, D), lambda i: (i, 0))
    out_specs = [row]
    out_shape = [jax.ShapeDtypeStruct((T, D), F32)]
    if emit_next:
        out_specs.append(row)
        out_shape.append(jax.ShapeDtypeStruct((T, D), BF16))
    kern = functools.partial(_ffn_kernel, tiles_per_seq=seq_len // tm, emit_next=emit_next)
    return pl.pallas_call(
        kern,
        grid=(T // tm,),
        in_specs=[
            row,
            pl.BlockSpec((halo_rows, D), lambda i: (jnp.maximum(i * per_tile - 1, 0), 0)),
            pl.BlockSpec((halo_rows, D), lambda i: (jnp.minimum((i + 1) * per_tile, n_halo - 1), 0)),
            row,
            _const_spec(w_in16.shape), _const_spec(conv_w.shape),
            _const_spec(cbt.shape), _const_spec(wo.shape),
            _const_spec((1, D)), _const_spec((1, D)),
        ],
        out_specs=out_specs,
        out_shape=out_shape,
        scratch_shapes=[pltpu.VMEM((nc, tm, cw), BF16)],
        compiler_params=_params("parallel"),
        name="conv_ffn",
    )(hn, hn, hn, x2d, w_in16, conv_w, cbt, wo, g_post[None, :], g_next[None, :])


def _hgrn_in_kernel(hn_ref, w_ref, o_ref):
    res = _dot(hn_ref[...], w_ref[...])
    for cb in range(o_ref.shape[0]):
        o_ref[cb] = res[:, cb * V7X_LANES:(cb + 1) * V7X_LANES]


def _hgrn_in_proj(hn, w_in, *, tm=1024, tn=1024):
    T, D = hn.shape
    N = w_in.shape[1]
    return pl.pallas_call(
        _hgrn_in_kernel,
        grid=(N // tn, T // tm),
        in_specs=[pl.BlockSpec((tm, D), lambda n, i: (i, 0)),
                  pl.BlockSpec((D, tn), lambda n, i: (0, n))],
        out_specs=pl.BlockSpec((tn // V7X_LANES, tm, V7X_LANES), lambda n, i: (n, i, 0)),
        out_shape=jax.ShapeDtypeStruct((N // V7X_LANES, T, V7X_LANES), F32),
        compiler_params=_params("parallel", "parallel"),
        name="hgrn_in_proj",
    )(hn, w_in.astype(BF16))


def _scan_gates(lbl_ref, q_ref, f_ref, d, h, layer_idx):
    depth = lbl_ref.shape[1]
    logits = [lbl_ref[d, i, h] for i in range(depth)]
    mx = functools.reduce(jnp.maximum, logits)
    es = [jnp.exp(l - mx) for l in logits]
    tot = functools.reduce(lambda a, b: a + b, es)
    picked = es[1:layer_idx + 1]
    lb = functools.reduce(lambda a, b: a + b, picked) / tot if picked else jnp.zeros_like(tot)
    one_m_lb = 1.0 - lb
    sig_f, sig_nf = _sigmoid_pair(f_ref[h])
    qx = q_ref[h]
    q = qx * _sigmoid_pair(qx)[0] * (float(HGRN_EXPAND) ** -0.5)
    return q, lb + one_m_lb * sig_f, one_m_lb * sig_nf


def _half_chunk_decay(f):
    nb = SCAN_CHUNK // V7X_SUBLANES
    out = []
    for half in range(2):
        bl = [f[V7X_SUBLANES * j:V7X_SUBLANES * (j + 1), :] for j in range(half * nb // 2, (half + 1) * nb // 2)]
        p = functools.reduce(lambda a, b: a * b, bl)
        for shift in (4, 2, 1):
            p = p * pltpu.roll(p, shift, 0)
        out.append(p)
    return out


def _scan_core(q, f, k, v, mask_ref, o_ref, st_ref, d, h, *, rev, fast):
    nb = SCAN_CHUNK // V7X_SUBLANES
    n_levels = mask_ref.shape[1] - 1

    sub = lax.broadcasted_iota(jnp.int32, (V7X_SUBLANES, V7X_LANES), 0)
    il = (V7X_SUBLANES - 1 - sub) if rev else sub

    def blocks(a):
        bl = [a[V7X_SUBLANES * j:V7X_SUBLANES * (j + 1), :] for j in range(nb)]
        return bl[::-1] if rev else bl

    def unblocks(bl):
        return jnp.concatenate(bl[::-1] if rev else bl, axis=0)

    def row(x, pos):
        u = (V7X_SUBLANES - 1 - pos) if rev else pos
        return jnp.broadcast_to(x[u:u + 1, :], x.shape)

    def at_prev(x):
        return pltpu.roll(x, (V7X_SUBLANES - 1) if rev else 1, 0)

    def at_next(x):
        return pltpu.roll(x, 1 if rev else (V7X_SUBLANES - 1), 0)

    qb, kb = blocks(q), blocks(k)
    hi1, hi2, hi4 = [(il & c) != 0 for c in (1, 2, 4)]

    def level_operands(F, G):
        qh = unblocks([a * b for a, b in zip(qb, F)]).astype(BF16)
        kh = unblocks([a * b for a, b in zip(kb, G)] if G is not None else kb).astype(BF16)
        return qh, kh

    last = V7X_SUBLANES - 1

    def next_prefix(F, lvl):
        if lvl == 0:
            return [x * jnp.where(hi1, at_prev(x), 1.0) for x in F]
        if lvl == 1:
            return [x * jnp.where(hi2, jnp.where(hi4, row(x, 5), row(x, 1)), 1.0) for x in F]
        if lvl == 2:
            return [x * jnp.where(hi4, row(x, 3), 1.0) for x in F]
        cb = 2 ** (lvl - 3)
        return [F[b] * row(F[(b & ~(2 * cb - 1)) + cb - 1], last) if b & cb else F[b]
                for b in range(nb)]

    def next_level(F, G, lvl):
        if lvl == 0:
            newG = [jnp.where(hi1, 1.0, at_next(x)) for x in F]
        elif lvl == 1:
            newG = [g * jnp.where(hi2, 1.0, jnp.where(hi4, row(x, 7), row(x, 3))) for x, g in zip(F, G)]
        elif lvl == 2:
            newG = [g * jnp.where(hi4, 1.0, row(x, 7)) for x, g in zip(F, G)]
        else:
            cb = 2 ** (lvl - 3)
            newG = [G[b] if b & cb else G[b] * row(F[(b & ~(2 * cb - 1)) + 2 * cb - 1], last)
                    for b in range(nb)]
        return next_prefix(F, lvl), newG

    F, G = blocks(f), None
    if fast:
        for lvl in range(n_levels - 1):
            F = next_prefix(F, lvl)
        half = nb // 2
        tot1, tot2 = row(F[half - 1], last), row(F[nb - 1], last)
        inv = [1.0 / x for x in F]
        inv_tot1 = 1.0 / tot1
        q_fac = [F[b] * inv_tot1 if b < half else F[b] for b in range(nb)]
        k_fac = [inv[b] * tot1 if b < half else inv[b] for b in range(nb)]
        qh = unblocks([a * b for a, b in zip(qb, q_fac)]).astype(BF16)
        kh = unblocks([a * b for a, b in zip(kb, k_fac)]).astype(BF16)
        s_acc = jnp.where(mask_ref[d, n_levels] != 0.0, _dot_nt(qh, kh), 0.0)
        tot = tot1 * tot2
        G = [inv[b] * (tot if b < half else tot2) for b in range(nb)]
        F = [F[b] if b < half else F[b] * tot1 for b in range(nb)]
    else:
        s_acc = None
        for lvl in range(n_levels):
            qh, kh = level_operands(F, G)
            term = _dot_nt(qh, kh) * mask_ref[d, lvl]
            s_acc = term if s_acc is None else s_acc + term
            F, G = next_level(F, G, lvl)

    q_dec, k_dec = level_operands(F, G)
    st = st_ref[d, h]
    v16 = v.astype(BF16)
    diag = jnp.sum(q * k, axis=-1, keepdims=True)
    o = _dot_nt(q_dec, st.astype(BF16)) + _dot(s_acc.astype(BF16), v16) + diag * v
    o_ref[h] = o.astype(o_ref.dtype)
    f_total = row(F[nb - 1], last)[0:1, :]
    st_ref[d, h] = st * f_total + _dot_tn(v16, k_dec)


def _scan_level_masks():
    n = SCAN_CHUNK
    pos = np.arange(n)
    n_levels = n.bit_length() - 1
    out = np.zeros((2, n_levels + 1, n, n), np.float32)
    for d in range(2):
        idx = (n - 1 - pos) if d == 1 else pos
        ti, si = idx[:, None], idx[None, :]
        lvl, c = 0, 1
        while c < n:
            out[d, lvl] = ((ti & c) != 0) & ((si & c) == 0) & ((ti // (2 * c)) == (si // (2 * c)))
            lvl, c = lvl + 1, 2 * c
        out[d, n_levels] = out[d, :n_levels].sum(axis=0)
    return out


def _hgrn_scan_kernel(lbl_ref, mask_ref, qf_ref, ff_ref, vf_ref, qb_ref, fb_ref, vb_ref, of_ref,
                      ob_ref, st_ref, gate_ref, *, layer_idx):
    @pl.when(pl.program_id(1) == 0)
    def _():
        st_ref[...] = jnp.zeros_like(st_ref)

    heads = qf_ref.shape[0]
    dirs = ((qf_ref, ff_ref, vf_ref, of_ref), (qb_ref, fb_ref, vb_ref, ob_ref))

    def gate_head(h, weakest):
        for d, (q_ref, f_ref, _, _) in enumerate(dirs):
            q, f, k = _scan_gates(lbl_ref, q_ref, f_ref, d, h, layer_idx)
            gate_ref[0, d, h] = q
            gate_ref[1, d, h] = f
            gate_ref[2, d, h] = k
            for p in _half_chunk_decay(f):
                weakest = jnp.minimum(weakest, p)
        return weakest

    weakest = lax.fori_loop(0, heads, gate_head,
                            jnp.ones((V7X_SUBLANES, V7X_LANES), F32), unroll=heads)
    fast_ok = jnp.min(weakest) >= SCAN_FAST_MIN_HALF_DECAY

    def one(h, d, fast):
        _, _, v_ref, o_ref = dirs[d]
        _scan_core(gate_ref[0, d, h], gate_ref[1, d, h], gate_ref[2, d, h], v_ref[h], mask_ref,
                   o_ref, st_ref, d, h, rev=d == 1, fast=fast)

    @pl.when(fast_ok)
    def _():
        for h in range(heads):
            for d in range(2):
                one(h, d, True)

    @pl.when(jnp.logical_not(fast_ok))
    def _():
        def head(h, carry):
            for d in range(2):
                one(h, d, False)
            return carry
        lax.fori_loop(0, heads, head, 0, unroll=2)


def _hgrn_scan(proj, lb_logits, *, batch, seq_len, layer_idx, heads):
    T = proj.shape[1]
    hd = proj.shape[2]
    nchunk = seq_len // SCAN_CHUNK
    p5 = proj.reshape(5, heads, batch, seq_len, hd)
    depth = lb_logits.shape[1]
    lbl = lb_logits.reshape(2, depth, heads, 1, hd)
    masks = jnp.asarray(_scan_level_masks())

    def sec(section, backward):
        if backward:
            return pl.BlockSpec((None, heads, None, SCAN_CHUNK, hd),
                                lambda b, j: (section, 0, b, nchunk - 1 - j, 0))
        return pl.BlockSpec((None, heads, None, SCAN_CHUNK, hd), lambda b, j: (section, 0, b, j, 0))

    out_f = pl.BlockSpec((heads, None, SCAN_CHUNK, hd), lambda b, j: (0, b, j, 0))
    out_b = pl.BlockSpec((heads, None, SCAN_CHUNK, hd), lambda b, j: (0, b, nchunk - 1 - j, 0))
    o_sds = jax.ShapeDtypeStruct((heads, batch, seq_len, hd), BF16)
    of, ob = pl.pallas_call(
        functools.partial(_hgrn_scan_kernel, layer_idx=layer_idx),
        grid=(batch, nchunk),
        in_specs=[_const_spec(lbl.shape), _const_spec(masks.shape),
                  sec(0, False), sec(1, False), sec(3, False),
                  sec(0, True), sec(2, True), sec(3, True)],
        out_specs=[out_f, out_b],
        out_shape=[o_sds, o_sds],
        scratch_shapes=[pltpu.VMEM((2, heads, hd, hd), F32),
                        pltpu.VMEM((3, 2, heads, SCAN_CHUNK, hd), F32)],
        compiler_params=_params("arbitrary", "arbitrary"),
        name="hgrn_scan",
    )(lbl, masks, p5, p5, p5, p5, p5, p5)
    return of.reshape(heads, T, hd), ob.reshape(heads, T, hd)


def kernel(x, positions, pre_mix_norm, post_mix_norm, pre_ffn_norm, post_ffn_norm, mla_w_in, mla_q_norm, mla_w_q_up, mla_kv_norm, mla_w_kv_up, mla_w_out, hgrn_w_in, hgrn_lb_logits, hgrn_out_norm, hgrn_w_out, ffn_w_in, ffn_conv_w, ffn_conv_b, ffn_w_out):
    B, S, D = x.shape
    T = B * S
    depth = pre_mix_norm.shape[0]
    n_mixers = 2
    hgrn_heads = D // HGRN_EXPAND
    x2d = x.reshape(T, D)
    hn = None
    tm_out = 512
    for l in range(depth):
        j = l // n_mixers
        if l % n_mixers == 0:
            q, k, v = _mla_proj(x2d.reshape(B, S, D), positions, pre_mix_norm[l], mla_w_in[j],
                                mla_q_norm[j], mla_w_q_up[j], mla_kv_norm[j], mla_w_kv_up[j])
            o = _attention(q, k, v).reshape(T, -1)
            x2d, hn = _mix_out_call(
                _mla_out_kernel, [o], [pl.BlockSpec((tm_out, o.shape[1]), lambda i: (i, 0))],
                x2d, mla_w_out[j], post_mix_norm[l], pre_ffn_norm[l], tm_out, "mla_out")
        else:
            proj = _hgrn_in_proj(hn, hgrn_w_in[j])
            of, ob = _hgrn_scan(proj, hgrn_lb_logits, batch=B, seq_len=S, layer_idx=l,
                                heads=hgrn_heads)
            hd = proj.shape[2]
            head_rows = pl.BlockSpec((hgrn_heads, tm_out, hd), lambda i: (0, i, 0))
            gate_rows = pl.BlockSpec((hgrn_heads, tm_out, hd), lambda i: (4, i, 0))
            x2d, hn = _mix_out_call(
                _hgrn_out_kernel,
                [of, ob, proj, hgrn_out_norm[j].reshape(hgrn_heads, 1, hd)],
                [head_rows, head_rows, gate_rows, _const_spec((hgrn_heads, 1, hd))],
                x2d, hgrn_w_out[j], post_mix_norm[l], pre_ffn_norm[l], tm_out, "hgrn_out")
        last = l == depth - 1
        g_next = pre_mix_norm[l] if last else pre_mix_norm[l + 1]
        outs = _conv_ffn(hn, x2d, S, ffn_w_in[l], ffn_conv_w[l], ffn_conv_b[l], ffn_w_out[l],
                         post_ffn_norm[l], g_next, emit_next=not last)
        if last:
            (x2d,) = outs
        else:
            x2d, hn = outs
    return x2d.reshape(B, S, D)
```

```python
import functools

import jax
import jax.numpy as jnp
import numpy as np
from jax import lax
from jax.experimental import pallas as pl
from jax.experimental.pallas import tpu as pltpu

F32 = jnp.float32
BF16 = jnp.bfloat16

EPS = 1e-6
LOG2_E = 1.4426950408889634
ROPE_THETA = 10000.0
MLA_NOPE = 128
MLA_ROPE = 64
MLA_V = 128
HGRN_EXPAND = 128
SCAN_CHUNK = 64
SCAN_FAST_MIN_HALF_DECAY = 2.0 ** -100
FFN_COL_CHUNK = 256

V7X_LANES = 128
V7X_SUBLANES = 8
V7X_BF16_ROWS = 16
V7X_VMEM_LIMIT = 56 * 1024 * 1024


def _rms(x, gain):
    ms = jnp.mean(x * x, axis=-1, keepdims=True)
    return x * lax.rsqrt(ms + EPS) * gain


def _sigmoid_pair(x):
    e = jnp.exp(-jnp.abs(x))
    r = 1.0 / (1.0 + e)
    er = e * r
    pos = x >= 0
    return jnp.where(pos, r, er), jnp.where(pos, er, r)


def _silu(x):
    return x / (1.0 + jnp.exp(-x))


def _dot(a, b):
    return jnp.dot(a, b, preferred_element_type=F32)


def _dot_nt(a, b):
    return lax.dot_general(a, b, (((1,), (1,)), ((), ())), preferred_element_type=F32)


def _dot_tn(a, b):
    return lax.dot_general(a, b, (((0,), (0,)), ((), ())), preferred_element_type=F32)


def _params(*sem):
    return pltpu.CompilerParams(dimension_semantics=sem, vmem_limit_bytes=V7X_VMEM_LIMIT)


def _const_spec(shape):
    nd = len(shape)
    return pl.BlockSpec(shape, lambda *_: (0,) * nd)


def _mla_proj_kernel(x_ref, pos_ref, g_pre_ref, w_in_ref, g_q_ref, w_q_ref, g_kv_ref,
                     w_kv_ref, invf_ref, sgn_ref, q_ref, k_ref, v_ref, *, q_lora, kv_lora, heads):
    hn = _rms(x_ref[...], g_pre_ref[...]).astype(BF16)
    proj = _dot(hn, w_in_ref[...])
    c_q = proj[:, :q_lora]
    c_kv = proj[:, q_lora:q_lora + kv_lora]
    kr = proj[:, q_lora + kv_lora:q_lora + kv_lora + MLA_ROPE]
    kr_sw = proj[:, q_lora + kv_lora + MLA_ROPE:]

    ang = pos_ref[...].astype(F32) * invf_ref[...]
    cos2 = jnp.cos(ang)
    sin2 = jnp.sin(ang) * sgn_ref[...]

    q_all = _dot(_rms(c_q, g_q_ref[...]).astype(BF16), w_q_ref[...])
    kv = _dot(_rms(c_kv, g_kv_ref[...]).astype(BF16), w_kv_ref[...])

    n_nope = heads * MLA_NOPE
    n_rope = heads * MLA_ROPE
    k_rot = (kr * cos2[:, :MLA_ROPE] + kr_sw * sin2[:, :MLA_ROPE]).astype(BF16)
    for pair in range(heads // 2):
        lo = n_nope + pair * V7X_LANES
        q_rot = (q_all[:, lo:lo + V7X_LANES] * cos2
                 + q_all[:, lo + n_rope:lo + n_rope + V7X_LANES] * sin2).astype(BF16)
        for sub in range(2):
            h = 2 * pair + sub
            q_ref[h, :, MLA_NOPE:] = q_rot[:, sub * MLA_ROPE:(sub + 1) * MLA_ROPE]
    for h in range(heads):
        q_ref[h, :, :MLA_NOPE] = q_all[:, h * MLA_NOPE:(h + 1) * MLA_NOPE].astype(BF16)
        k_ref[h, :, :MLA_NOPE] = kv[:, h * MLA_NOPE:(h + 1) * MLA_NOPE].astype(BF16)
        k_ref[h, :, MLA_NOPE:] = k_rot
        v_ref[h, :, :MLA_V] = kv[:, n_nope + h * MLA_V:n_nope + (h + 1) * MLA_V].astype(BF16)
        v_ref[h, :, MLA_V:] = jnp.ones((v_ref.shape[1], MLA_V), BF16)


def _mla_proj(x, positions, g_pre, w_in, g_q, w_q_up, g_kv, w_kv_up, *, tm=512):
    B, S, D = x.shape
    q_lora = g_q.shape[0]
    kv_lora = g_kv.shape[0]
    heads = w_q_up.shape[1] // (MLA_NOPE + MLA_ROPE)
    half = MLA_ROPE // 2
    qk_dim = MLA_NOPE + MLA_ROPE

    kr_lo = q_lora + kv_lora
    w_in_x = jnp.concatenate(
        [w_in, w_in[:, kr_lo + half:kr_lo + MLA_ROPE], w_in[:, kr_lo:kr_lo + half]], axis=1).astype(BF16)
    wq3 = w_q_up.reshape(q_lora, heads, qk_dim)
    rope = wq3[:, :, MLA_NOPE:]
    rope_sw = jnp.concatenate([rope[:, :, half:], rope[:, :, :half]], axis=-1)
    w_q_x = jnp.concatenate(
        [wq3[:, :, :MLA_NOPE].reshape(q_lora, -1), rope.reshape(q_lora, -1),
         rope_sw.reshape(q_lora, -1)], axis=1).astype(BF16)
    wkv3 = w_kv_up.reshape(kv_lora, heads, MLA_NOPE + MLA_V)
    w_kv_x = jnp.concatenate(
        [wkv3[:, :, :MLA_NOPE].reshape(kv_lora, -1), wkv3[:, :, MLA_NOPE:].reshape(kv_lora, -1)],
        axis=1).astype(BF16)

    inv_freq = 1.0 / (ROPE_THETA ** (jnp.arange(0, MLA_ROPE, 2, dtype=F32) / MLA_ROPE))
    invf = jnp.tile(inv_freq, V7X_LANES // half)[None, :]
    sgn = jnp.tile(jnp.concatenate([-jnp.ones((half,), F32), jnp.ones((half,), F32)]),
                   V7X_LANES // MLA_ROPE)[None, :]
    scale = float(qk_dim) ** -0.5 * LOG2_E

    kern = functools.partial(_mla_proj_kernel, q_lora=q_lora, kv_lora=kv_lora, heads=heads)
    head_out = lambda width: pl.BlockSpec((None, heads, tm, width), lambda b, i: (b, 0, i, 0))
    return pl.pallas_call(
        kern,
        grid=(B, S // tm),
        in_specs=[
            pl.BlockSpec((None, tm, D), lambda b, i: (b, i, 0)),
            pl.BlockSpec((None, tm, 1), lambda b, i: (b, i, 0)),
            _const_spec((1, D)),
            _const_spec(w_in_x.shape),
            _const_spec((1, q_lora)),
            _const_spec(w_q_x.shape),
            _const_spec((1, kv_lora)),
            _const_spec(w_kv_x.shape),
            _const_spec((1, V7X_LANES)),
            _const_spec((1, V7X_LANES)),
        ],
        out_specs=[head_out(qk_dim), head_out(qk_dim), head_out(2 * MLA_V)],
        out_shape=[
            jax.ShapeDtypeStruct((B, heads, S, qk_dim), BF16),
            jax.ShapeDtypeStruct((B, heads, S, qk_dim), BF16),
            jax.ShapeDtypeStruct((B, heads, S, 2 * MLA_V), BF16),
        ],
        compiler_params=_params("parallel", "parallel"),
        name="mla_proj",
    )(x, positions.reshape(B, S, 1), g_pre[None, :], w_in_x, (g_q * scale)[None, :], w_q_x,
      g_kv[None, :], w_kv_x, invf, sgn)


def _attn_kernel(q_ref, k_ref, v_ref, o_ref, *, tq):
    def q_tile(i, carry):
        r0 = pl.multiple_of(i * tq, tq)
        s = _dot_nt(q_ref[pl.ds(r0, tq), :], k_ref[...])
        p = jnp.exp2(s - jnp.max(s, axis=-1, keepdims=True))
        o = _dot(p.astype(BF16), v_ref[...])
        dv = o_ref.shape[1]
        o_ref[pl.ds(r0, tq), :] = (o[:, :dv] / o[:, dv:]).astype(o_ref.dtype)
        return carry

    lax.fori_loop(0, q_ref.shape[0] // tq, q_tile, 0, unroll=True)


def _attention(q, k, v, *, tq=256):
    B, H, S, dk = q.shape
    dv = v.shape[-1] // 2
    per_head = lambda width: pl.BlockSpec((None, None, S, width), lambda b, h: (b, h, 0, 0))
    return pl.pallas_call(
        functools.partial(_attn_kernel, tq=tq),
        grid=(B, H),
        in_specs=[per_head(dk), per_head(dk), per_head(2 * dv)],
        out_specs=pl.BlockSpec((None, S, dv), lambda b, h: (b, 0, h)),
        out_shape=jax.ShapeDtypeStruct((B, S, H * dv), BF16),
        compiler_params=_params("parallel", "parallel"),
        name="mla_attention",
    )(q, k, v)


def _mix_epilogue(a_bf16, x_ref, w_ref, g_post_ref, g_ffn_ref, x_out_ref, hn_out_ref):
    m = _dot(a_bf16, w_ref[...])
    x1 = x_ref[...] + _rms(m, g_post_ref[...])
    x_out_ref[...] = x1
    hn_out_ref[...] = _rms(x1, g_ffn_ref[...]).astype(BF16)


def _mla_out_kernel(o_ref, x_ref, w_ref, g_post_ref, g_ffn_ref, x_out_ref, hn_out_ref):
    _mix_epilogue(o_ref[...], x_ref, w_ref, g_post_ref, g_ffn_ref, x_out_ref, hn_out_ref)


def _hgrn_out_kernel(of_ref, ob_ref, gate_ref, g_head_ref, x_ref, w_ref, g_post_ref, g_ffn_ref,
                     x_out_ref, hn_out_ref):
    heads = of_ref.shape[0]
    cols = []
    for h in range(heads):
        o = of_ref[h].astype(F32) + ob_ref[h].astype(F32)
        gx = gate_ref[h]
        cols.append((_rms(o, g_head_ref[h]) * _silu(gx)).astype(BF16))
    _mix_epilogue(jnp.concatenate(cols, axis=1), x_ref, w_ref, g_post_ref, g_ffn_ref,
                  x_out_ref, hn_out_ref)


def _mix_out_call(kern, lead_args, lead_specs, x2d, w_out, g_post, g_ffn, tm, name):
    T, D = x2d.shape
    row = pl.BlockSpec((tm, D), lambda i: (i, 0))
    return pl.pallas_call(
        kern,
        grid=(T // tm,),
        in_specs=lead_specs + [row, _const_spec(w_out.shape), _const_spec((1, D)), _const_spec((1, D))],
        out_specs=[row, row],
        out_shape=[jax.ShapeDtypeStruct((T, D), F32), jax.ShapeDtypeStruct((T, D), BF16)],
        compiler_params=_params("parallel"),
        name=name,
    )(*lead_args, x2d, w_out.astype(BF16), g_post[None, :], g_ffn[None, :])


def _gelu_tanh(x):
    c = 0.7978845608028654
    return 0.5 * x * (1.0 + jnp.tanh(c * (x + 0.044715 * (x * x * x))))


def _ffn_kernel(hn_ref, hprev_ref, hnext_ref, x_ref, w_in_ref, cw_ref, cb_ref, wo_ref,
                g_post_ref, g_next_ref, x_out_ref, *rest, tiles_per_seq, emit_next):
    if emit_next:
        hn_out_ref, act_ref = rest
    else:
        (act_ref,) = rest
    i = pl.program_id(0)
    tm = hn_ref.shape[0]
    hn = hn_ref[...]
    halo = jnp.concatenate([hprev_ref[...], hnext_ref[...]], axis=0)
    t_in_seq = i % tiles_per_seq
    seq_start = t_in_seq == 0
    seq_end = t_in_seq == tiles_per_seq - 1
    rows = lax.broadcasted_iota(jnp.int32, (tm, 1), 0)

    nc, _, cw_cols = act_ref.shape
    d_ff = nc * cw_cols

    def col_chunk(c):
        cols = slice(c * cw_cols, (c + 1) * cw_cols)
        wg = w_in_ref[:, cols]
        g = _dot(hn, wg)
        gh = _dot(halo, wg)
        val = _dot(hn, w_in_ref[:, d_ff + c * cw_cols:d_ff + (c + 1) * cw_cols])
        g_before = jnp.where(seq_start, 0.0, gh[V7X_BF16_ROWS - 1:V7X_BF16_ROWS, :])
        g_after = jnp.where(seq_end, 0.0, gh[V7X_BF16_ROWS:V7X_BF16_ROWS + 1, :])
        g_m1 = jnp.where(rows == 0, g_before, pltpu.roll(g, 1, 0))
        g_p1 = jnp.where(rows == tm - 1, g_after, pltpu.roll(g, tm - 1, 0))
        cw = cw_ref[:, cols]
        conv = g_m1 * cw[0:1, :] + g * cw[1:2, :] + g_p1 * cw[2:3, :] + cb_ref[:, cols]
        act_ref[c] = (_gelu_tanh(conv) * val).astype(BF16)

    for c in range(nc):
        col_chunk(c)
    f = _dot(act_ref[0], wo_ref[:cw_cols, :])
    for c in range(1, nc):
        f = f + _dot(act_ref[c], wo_ref[c * cw_cols:(c + 1) * cw_cols, :])
    x2 = x_ref[...] + _rms(f, g_post_ref[...])
    x_out_ref[...] = x2
    if emit_next:
        hn_out_ref[...] = _rms(x2, g_next_ref[...]).astype(BF16)


def _conv_ffn(hn, x2d, seq_len, w_in, conv_w, conv_b, w_out, g_post, g_next, *, emit_next, tm=512):
    T, D = x2d.shape
    d_ff = w_out.shape[0]
    cw = FFN_COL_CHUNK
    nc = d_ff // cw
    w_in16 = w_in.astype(BF16)
    wo = w_out.astype(BF16)
    cbt = conv_b[None, :]
    halo_rows = V7X_BF16_ROWS
    per_tile = tm // halo_rows
    n_halo = T // halo_rows
    row = pl.BlockSpec((tm, D), lambda i: (i, 0))
    out_specs = [row]
    out_shape = [jax.ShapeDtypeStruct((T, D), F32)]
    if emit_next:
        out_specs.append(row)
        out_shape.append(jax.ShapeDtypeStruct((T, D), BF16))
    kern = functools.partial(_ffn_kernel, tiles_per_seq=seq_len // tm, emit_next=emit_next)
    return pl.pallas_call(
        kern,
        grid=(T // tm,),
        in_specs=[
            row,
            pl.BlockSpec((halo_rows, D), lambda i: (jnp.maximum(i * per_tile - 1, 0), 0)),
            pl.BlockSpec((halo_rows, D), lambda i: (jnp.minimum((i + 1) * per_tile, n_halo - 1), 0)),
            row,
            _const_spec(w_in16.shape), _const_spec(conv_w.shape),
            _const_spec(cbt.shape), _const_spec(wo.shape),
            _const_spec((1, D)), _const_spec((1, D)),
        ],
        out_specs=out_specs,
        out_shape=out_shape,
        scratch_shapes=[pltpu.VMEM((nc, tm, cw), BF16)],
        compiler_params=_params("parallel"),
        name="conv_ffn",
    )(hn, hn, hn, x2d, w_in16, conv_w, cbt, wo, g_post[None, :], g_next[None, :])


def _hgrn_in_kernel(hn_ref, w_ref, o_ref):
    res = _dot(hn_ref[...], w_ref[...])
    for cb in range(o_ref.shape[0]):
        o_ref[cb] = res[:, cb * V7X_LANES:(cb + 1) * V7X_LANES]


def _hgrn_in_proj(hn, w_in, *, tm=1024, tn=1024):
    T, D = hn.shape
    N = w_in.shape[1]
    return pl.pallas_call(
        _hgrn_in_kernel,
        grid=(N // tn, T // tm),
        in_specs=[pl.BlockSpec((tm, D), lambda n, i: (i, 0)),
                  pl.BlockSpec((D, tn), lambda n, i: (0, n))],
        out_specs=pl.BlockSpec((tn // V7X_LANES, tm, V7X_LANES), lambda n, i: (n, i, 0)),
        out_shape=jax.ShapeDtypeStruct((N // V7X_LANES, T, V7X_LANES), F32),
        compiler_params=_params("parallel", "parallel"),
        name="hgrn_in_proj",
    )(hn, w_in.astype(BF16))


def _scan_gates(lbl_ref, q_ref, f_ref, d, h, layer_idx):
    depth = lbl_ref.shape[1]
    logits = [lbl_ref[d, i, h] for i in range(depth)]
    mx = functools.reduce(jnp.maximum, logits)
    es = [jnp.exp(l - mx) for l in logits]
    tot = functools.reduce(lambda a, b: a + b, es)
    picked = es[1:layer_idx + 1]
    lb = functools.reduce(lambda a, b: a + b, picked) / tot if picked else jnp.zeros_like(tot)
    one_m_lb = 1.0 - lb
    sig_f, sig_nf = _sigmoid_pair(f_ref[h])
    qx = q_ref[h]
    q = _silu(qx) * (float(HGRN_EXPAND) ** -0.5)
    return q, lb + one_m_lb * sig_f, one_m_lb * sig_nf


def _half_chunk_decay(f):
    nb = SCAN_CHUNK // V7X_SUBLANES
    out = []
    for half in range(2):
        bl = [f[V7X_SUBLANES * j:V7X_SUBLANES * (j + 1), :] for j in range(half * nb // 2, (half + 1) * nb // 2)]
        p = functools.reduce(lambda a, b: a * b, bl)
        for shift in (4, 2, 1):
            p = p * pltpu.roll(p, shift, 0)
        out.append(p)
    return out


def _scan_core(q, f, k, v, mask_ref, o_ref, st_ref, d, h, *, rev, fast):
    nb = SCAN_CHUNK // V7X_SUBLANES
    n_levels = mask_ref.shape[1] - 1

    sub = lax.broadcasted_iota(jnp.int32, (V7X_SUBLANES, V7X_LANES), 0)
    il = (V7X_SUBLANES - 1 - sub) if rev else sub

    def blocks(a):
        bl = [a[V7X_SUBLANES * j:V7X_SUBLANES * (j + 1), :] for j in range(nb)]
        return bl[::-1] if rev else bl

    def unblocks(bl):
        return jnp.concatenate(bl[::-1] if rev else bl, axis=0)

    def row(x, pos):
        u = (V7X_SUBLANES - 1 - pos) if rev else pos
        return jnp.broadcast_to(x[u:u + 1, :], x.shape)

    def at_prev(x):
        return pltpu.roll(x, (V7X_SUBLANES - 1) if rev else 1, 0)

    def at_next(x):
        return pltpu.roll(x, 1 if rev else (V7X_SUBLANES - 1), 0)

    qb, kb = blocks(q), blocks(k)
    hi1, hi2, hi4 = [(il & c) != 0 for c in (1, 2, 4)]

    def level_operands(F, G):
        qh = unblocks([a * b for a, b in zip(qb, F)]).astype(BF16)
        kh = unblocks([a * b for a, b in zip(kb, G)] if G is not None else kb).astype(BF16)
        return qh, kh

    last = V7X_SUBLANES - 1

    def next_prefix(F, lvl):
        if lvl == 0:
            return [x * jnp.where(hi1, at_prev(x), 1.0) for x in F]
        if lvl == 1:
            return [x * jnp.where(hi2, jnp.where(hi4, row(x, 5), row(x, 1)), 1.0) for x in F]
        if lvl == 2:
            return [x * jnp.where(hi4, row(x, 3), 1.0) for x in F]
        cb = 2 ** (lvl - 3)
        return [F[b] * row(F[(b & ~(2 * cb - 1)) + cb - 1], last) if b & cb else F[b]
                for b in range(nb)]

    def next_level(F, G, lvl):
        if lvl == 0:
            newG = [jnp.where(hi1, 1.0, at_next(x)) for x in F]
        elif lvl == 1:
            newG = [g * jnp.where(hi2, 1.0, jnp.where(hi4, row(x, 7), row(x, 3))) for x, g in zip(F, G)]
        elif lvl == 2:
            newG = [g * jnp.where(hi4, 1.0, row(x, 7)) for x, g in zip(F, G)]
        else:
            cb = 2 ** (lvl - 3)
            newG = [G[b] if b & cb else G[b] * row(F[(b & ~(2 * cb - 1)) + 2 * cb - 1], last)
                    for b in range(nb)]
        return next_prefix(F, lvl), newG

    F, G = blocks(f), None
    if fast:
        for lvl in range(n_levels - 1):
            F = next_prefix(F, lvl)
        half = nb // 2
        tot1, tot2 = row(F[half - 1], last), row(F[nb - 1], last)
        inv = [1.0 / x for x in F]
        inv_tot1 = 1.0 / tot1
        q_fac = [F[b] * inv_tot1 if b < half else F[b] for b in range(nb)]
        k_fac = [inv[b] * tot1 if b < half else inv[b] for b in range(nb)]
        qh = unblocks([a * b for a, b in zip(qb, q_fac)]).astype(BF16)
        kh = unblocks([a * b for a, b in zip(kb, k_fac)]).astype(BF16)
        s_acc = jnp.where(mask_ref[d, n_levels] != 0.0, _dot_nt(qh, kh), 0.0)
        tot = tot1 * tot2
        G = [inv[b] * (tot if b < half else tot2) for b in range(nb)]
        F = [F[b] if b < half else F[b] * tot1 for b in range(nb)]
    else:
        s_acc = None
        for lvl in range(n_levels):
            qh, kh = level_operands(F, G)
            term = _dot_nt(qh, kh) * mask_ref[d, lvl]
            s_acc = term if s_acc is None else s_acc + term
            F, G = next_level(F, G, lvl)

    q_dec, k_dec = level_operands(F, G)
    st = st_ref[d, h]
    v16 = v.astype(BF16)
    diag = jnp.sum(q * k, axis=-1, keepdims=True)
    o = _dot(q_dec, st.astype(BF16)) + _dot(s_acc.astype(BF16), v16) + diag * v
    o_ref[h] = o.astype(o_ref.dtype)
    f_total = row(F[nb - 1], last)[0:1, :]
    decay_rows = jnp.broadcast_to(f_total, (st.shape[1], st.shape[0])).T
    st_ref[d, h] = st * decay_rows + _dot_tn(k_dec, v16)


def _scan_level_masks():
    n = SCAN_CHUNK
    pos = np.arange(n)
    n_levels = n.bit_length() - 1
    out = np.zeros((2, n_levels + 1, n, n), np.float32)
    for d in range(2):
        idx = (n - 1 - pos) if d == 1 else pos
        ti, si = idx[:, None], idx[None, :]
        lvl, c = 0, 1
        while c < n:
            out[d, lvl] = ((ti & c) != 0) & ((si & c) == 0) & ((ti // (2 * c)) == (si // (2 * c)))
            lvl, c = lvl + 1, 2 * c
        out[d, n_levels] = out[d, :n_levels].sum(axis=0)
    return out


def _hgrn_scan_kernel(lbl_ref, mask_ref, qf_ref, ff_ref, vf_ref, qb_ref, fb_ref, vb_ref, of_ref,
                      ob_ref, st_ref, gate_ref, *, layer_idx):
    @pl.when(pl.program_id(1) == 0)
    def _():
        st_ref[...] = jnp.zeros_like(st_ref)

    heads = qf_ref.shape[0]
    dirs = ((qf_ref, ff_ref, vf_ref, of_ref), (qb_ref, fb_ref, vb_ref, ob_ref))

    def gate_head(h, weakest):
        for d, (q_ref, f_ref, _, _) in enumerate(dirs):
            q, f, k = _scan_gates(lbl_ref, q_ref, f_ref, d, h, layer_idx)
            gate_ref[0, d, h] = q
            gate_ref[1, d, h] = f
            gate_ref[2, d, h] = k
            for p in _half_chunk_decay(f):
                weakest = jnp.minimum(weakest, p)
        return weakest

    weakest = lax.fori_loop(0, heads, gate_head,
                            jnp.ones((V7X_SUBLANES, V7X_LANES), F32), unroll=heads)
    fast_ok = jnp.min(weakest) >= SCAN_FAST_MIN_HALF_DECAY

    def one(h, d, fast):
        _, _, v_ref, o_ref = dirs[d]
        _scan_core(gate_ref[0, d, h], gate_ref[1, d, h], gate_ref[2, d, h], v_ref[h], mask_ref,
                   o_ref, st_ref, d, h, rev=d == 1, fast=fast)

    @pl.when(fast_ok)
    def _():
        for h in range(heads):
            for d in range(2):
                one(h, d, True)

    @pl.when(jnp.logical_not(fast_ok))
    def _():
        def head(h, carry):
            for d in range(2):
                one(h, d, False)
            return carry
        lax.fori_loop(0, heads, head, 0, unroll=2)


def _hgrn_scan(proj, lb_logits, *, batch, seq_len, layer_idx, heads):
    T = proj.shape[1]
    hd = proj.shape[2]
    nchunk = seq_len // SCAN_CHUNK
    p5 = proj.reshape(5, heads, batch, seq_len, hd)
    depth = lb_logits.shape[1]
    lbl = lb_logits.reshape(2, depth, heads, 1, hd)
    masks = jnp.asarray(_scan_level_masks())

    def sec(section, backward):
        if backward:
            return pl.BlockSpec((None, heads, None, SCAN_CHUNK, hd),
                                lambda b, j: (section, 0, b, nchunk - 1 - j, 0))
        return pl.BlockSpec((None, heads, None, SCAN_CHUNK, hd), lambda b, j: (section, 0, b, j, 0))

    out_f = pl.BlockSpec((heads, None, SCAN_CHUNK, hd), lambda b, j: (0, b, j, 0))
    out_b = pl.BlockSpec((heads, None, SCAN_CHUNK, hd), lambda b, j: (0, b, nchunk - 1 - j, 0))
    o_sds = jax.ShapeDtypeStruct((heads, batch, seq_len, hd), BF16)
    of, ob = pl.pallas_call(
        functools.partial(_hgrn_scan_kernel, layer_idx=layer_idx),
        grid=(batch, nchunk),
        in_specs=[_const_spec(lbl.shape), _const_spec(masks.shape),
                  sec(0, False), sec(1, False), sec(3, False),
                  sec(0, True), sec(2, True), sec(3, True)],
        out_specs=[out_f, out_b],
        out_shape=[o_sds, o_sds],
        scratch_shapes=[pltpu.VMEM((2, heads, hd, hd), F32),
                        pltpu.VMEM((3, 2, heads, SCAN_CHUNK, hd), F32)],
        compiler_params=_params("arbitrary", "arbitrary"),
        name="hgrn_scan",
    )(lbl, masks, p5, p5, p5, p5, p5, p5)
    return of.reshape(heads, T, hd), ob.reshape(heads, T, hd)


def kernel(x, positions, pre_mix_norm, post_mix_norm, pre_ffn_norm, post_ffn_norm, mla_w_in, mla_q_norm, mla_w_q_up, mla_kv_norm, mla_w_kv_up, mla_w_out, hgrn_w_in, hgrn_lb_logits, hgrn_out_norm, hgrn_w_out, ffn_w_in, ffn_conv_w, ffn_conv_b, ffn_w_out):
    B, S, D = x.shape
    T = B * S
    depth = pre_mix_norm.shape[0]
    n_mixers = 2
    hgrn_heads = D // HGRN_EXPAND
    x2d = x.reshape(T, D)
    hn = None
    tm_out = 512
    for l in range(depth):
        j = l // n_mixers
        if l % n_mixers == 0:
            q, k, v = _mla_proj(x2d.reshape(B, S, D), positions, pre_mix_norm[l], mla_w_in[j],
                                mla_q_norm[j], mla_w_q_up[j], mla_kv_norm[j], mla_w_kv_up[j])
            o = _attention(q, k, v).reshape(T, -1)
            x2d, hn = _mix_out_call(
                _mla_out_kernel, [o], [pl.BlockSpec((tm_out, o.shape[1]), lambda i: (i, 0))],
                x2d, mla_w_out[j], post_mix_norm[l], pre_ffn_norm[l], tm_out, "mla_out")
        else:
            proj = _hgrn_in_proj(hn, hgrn_w_in[j])
            of, ob = _hgrn_scan(proj, hgrn_lb_logits, batch=B, seq_len=S, layer_idx=l,
                                heads=hgrn_heads)
            hd = proj.shape[2]
            head_rows = pl.BlockSpec((hgrn_heads, tm_out, hd), lambda i: (0, i, 0))
            gate_rows = pl.BlockSpec((hgrn_heads, tm_out, hd), lambda i: (4, i, 0))
            x2d, hn = _mix_out_call(
                _hgrn_out_kernel,
                [of, ob, proj, hgrn_out_norm[j].reshape(hgrn_heads, 1, hd)],
                [head_rows, head_rows, gate_rows, _const_spec((hgrn_heads, 1, hd))],
                x2d, hgrn_w_out[j], post_mix_norm[l], pre_ffn_norm[l], tm_out, "hgrn_out")
        last = l == depth - 1
        g_next = pre_mix_norm[l] if last else pre_mix_norm[l + 1]
        outs = _conv_ffn(hn, x2d, S, ffn_w_in[l], ffn_conv_w[l], ffn_conv_b[l], ffn_w_out[l],
                         post_ffn_norm[l], g_next, emit_next=not last)
        if last:
            (x2d,) = outs
        else:
            x2d, hn = outs
    return x2d.reshape(B, S, D)
```

```python
import functools

import jax
import jax.numpy as jnp
import numpy as np
from jax import lax
from jax.experimental import pallas as pl
from jax.experimental.pallas import tpu as pltpu

F32 = jnp.float32
BF16 = jnp.bfloat16

EPS = 1e-6
LOG2_E = 1.4426950408889634
ROPE_THETA = 10000.0
MLA_NOPE = 128
MLA_ROPE = 64
MLA_V = 128
HGRN_EXPAND = 128
SCAN_CHUNK = 64
SCAN_FAST_MIN_HALF_DECAY = 2.0 ** -100
FFN_COL_CHUNK = 256

V7X_LANES = 128
V7X_SUBLANES = 8
V7X_BF16_ROWS = 16
V7X_VMEM_LIMIT = 56 * 1024 * 1024


def _rms(x, gain):
    ms = jnp.mean(x * x, axis=-1, keepdims=True)
    return x * lax.rsqrt(ms + EPS) * gain


def _sigmoid_pair(x):
    e = jnp.exp(-jnp.abs(x))
    r = 1.0 / (1.0 + e)
    er = e * r
    pos = x >= 0
    return jnp.where(pos, r, er), jnp.where(pos, er, r)


def _silu(x):
    return x / (1.0 + jnp.exp(-x))


def _dot(a, b):
    return jnp.dot(a, b, preferred_element_type=F32)


def _dot_nt(a, b):
    return lax.dot_general(a, b, (((1,), (1,)), ((), ())), preferred_element_type=F32)


def _dot_tn(a, b):
    return lax.dot_general(a, b, (((0,), (0,)), ((), ())), preferred_element_type=F32)


def _params(*sem):
    return pltpu.CompilerParams(dimension_semantics=sem, vmem_limit_bytes=V7X_VMEM_LIMIT)


def _const_spec(shape):
    nd = len(shape)
    return pl.BlockSpec(shape, lambda *_: (0,) * nd)


def _mla_proj_kernel(x_ref, pos_ref, g_pre_ref, w_in_ref, g_q_ref, w_q_ref, g_kv_ref,
                     w_kv_ref, invf_ref, sgn_ref, q_ref, k_ref, v_ref, *, q_lora, kv_lora, heads):
    hn = _rms(x_ref[...], g_pre_ref[...]).astype(BF16)
    proj = _dot(hn, w_in_ref[...])
    c_q = proj[:, :q_lora]
    c_kv = proj[:, q_lora:q_lora + kv_lora]
    kr = proj[:, q_lora + kv_lora:q_lora + kv_lora + MLA_ROPE]
    kr_sw = proj[:, q_lora + kv_lora + MLA_ROPE:]

    ang = pos_ref[...].astype(F32) * invf_ref[...]
    cos2 = jnp.cos(ang)
    sin2 = jnp.sin(ang) * sgn_ref[...]

    q_all = _dot(_rms(c_q, g_q_ref[...]).astype(BF16), w_q_ref[...])
    kv = _dot(_rms(c_kv, g_kv_ref[...]).astype(BF16), w_kv_ref[...])

    n_nope = heads * MLA_NOPE
    n_rope = heads * MLA_ROPE
    k_rot = (kr * cos2[:, :MLA_ROPE] + kr_sw * sin2[:, :MLA_ROPE]).astype(BF16)
    for pair in range(heads // 2):
        lo = n_nope + pair * V7X_LANES
        q_rot = (q_all[:, lo:lo + V7X_LANES] * cos2
                 + q_all[:, lo + n_rope:lo + n_rope + V7X_LANES] * sin2).astype(BF16)
        for sub in range(2):
            h = 2 * pair + sub
            q_ref[h, :, MLA_NOPE:] = q_rot[:, sub * MLA_ROPE:(sub + 1) * MLA_ROPE]
    for h in range(heads):
        q_ref[h, :, :MLA_NOPE] = q_all[:, h * MLA_NOPE:(h + 1) * MLA_NOPE].astype(BF16)
        k_ref[h, :, :MLA_NOPE] = kv[:, h * MLA_NOPE:(h + 1) * MLA_NOPE].astype(BF16)
        k_ref[h, :, MLA_NOPE:] = k_rot
        v_ref[h, :, :MLA_V] = kv[:, n_nope + h * MLA_V:n_nope + (h + 1) * MLA_V].astype(BF16)
        v_ref[h, :, MLA_V:] = jnp.ones((v_ref.shape[1], MLA_V), BF16)


def _mla_proj(x, positions, g_pre, w_in, g_q, w_q_up, g_kv, w_kv_up, *, tm=512):
    B, S, D = x.shape
    q_lora = g_q.shape[0]
    kv_lora = g_kv.shape[0]
    heads = w_q_up.shape[1] // (MLA_NOPE + MLA_ROPE)
    half = MLA_ROPE // 2
    qk_dim = MLA_NOPE + MLA_ROPE

    kr_lo = q_lora + kv_lora
    w_in_x = jnp.concatenate(
        [w_in, w_in[:, kr_lo + half:kr_lo + MLA_ROPE], w_in[:, kr_lo:kr_lo + half]], axis=1).astype(BF16)
    wq3 = w_q_up.reshape(q_lora, heads, qk_dim)
    rope = wq3[:, :, MLA_NOPE:]
    rope_sw = jnp.concatenate([rope[:, :, half:], rope[:, :, :half]], axis=-1)
    w_q_x = jnp.concatenate(
        [wq3[:, :, :MLA_NOPE].reshape(q_lora, -1), rope.reshape(q_lora, -1),
         rope_sw.reshape(q_lora, -1)], axis=1).astype(BF16)
    wkv3 = w_kv_up.reshape(kv_lora, heads, MLA_NOPE + MLA_V)
    w_kv_x = jnp.concatenate(
        [wkv3[:, :, :MLA_NOPE].reshape(kv_lora, -1), wkv3[:, :, MLA_NOPE:].reshape(kv_lora, -1)],
        axis=1).astype(BF16)

    inv_freq = 1.0 / (ROPE_THETA ** (jnp.arange(0, MLA_ROPE, 2, dtype=F32) / MLA_ROPE))
    invf = jnp.tile(inv_freq, V7X_LANES // half)[None, :]
    sgn = jnp.tile(jnp.concatenate([-jnp.ones((half,), F32), jnp.ones((half,), F32)]),
                   V7X_LANES // MLA_ROPE)[None, :]
    scale = float(qk_dim) ** -0.5 * LOG2_E

    kern = functools.partial(_mla_proj_kernel, q_lora=q_lora, kv_lora=kv_lora, heads=heads)
    head_out = lambda width: pl.BlockSpec((None, heads, tm, width), lambda b, i: (b, 0, i, 0))
    return pl.pallas_call(
        kern,
        grid=(B, S // tm),
        in_specs=[
            pl.BlockSpec((None, tm, D), lambda b, i: (b, i, 0)),
            pl.BlockSpec((None, tm, 1), lambda b, i: (b, i, 0)),
            _const_spec((1, D)),
            _const_spec(w_in_x.shape),
            _const_spec((1, q_lora)),
            _const_spec(w_q_x.shape),
            _const_spec((1, kv_lora)),
            _const_spec(w_kv_x.shape),
            _const_spec((1, V7X_LANES)),
            _const_spec((1, V7X_LANES)),
        ],
        out_specs=[head_out(qk_dim), head_out(qk_dim), head_out(2 * MLA_V)],
        out_shape=[
            jax.ShapeDtypeStruct((B, heads, S, qk_dim), BF16),
            jax.ShapeDtypeStruct((B, heads, S, qk_dim), BF16),
            jax.ShapeDtypeStruct((B, heads, S, 2 * MLA_V), BF16),
        ],
        compiler_params=_params("parallel", "parallel"),
        name="mla_proj",
    )(x, positions.reshape(B, S, 1), g_pre[None, :], w_in_x, (g_q * scale)[None, :], w_q_x,
      g_kv[None, :], w_kv_x, invf, sgn)


def _attn_kernel(q_ref, k_ref, v_ref, o_ref, *, tq):
    def q_tile(i, carry):
        r0 = pl.multiple_of(i * tq, tq)
        s = _dot_nt(q_ref[pl.ds(r0, tq), :], k_ref[...])
        p = jnp.exp2(s - jnp.max(s, axis=-1, keepdims=True))
        o = _dot(p.astype(BF16), v_ref[...])
        dv = o_ref.shape[1]
        o_ref[pl.ds(r0, tq), :] = (o[:, :dv] / o[:, dv:]).astype(o_ref.dtype)
        return carry

    lax.fori_loop(0, q_ref.shape[0] // tq, q_tile, 0, unroll=True)


def _attention(q, k, v, *, tq=256):
    B, H, S, dk = q.shape
    dv = v.shape[-1] // 2
    per_head = lambda width: pl.BlockSpec((None, None, S, width), lambda b, h: (b, h, 0, 0))
    return pl.pallas_call(
        functools.partial(_attn_kernel, tq=tq),
        grid=(B, H),
        in_specs=[per_head(dk), per_head(dk), per_head(2 * dv)],
        out_specs=pl.BlockSpec((None, S, dv), lambda b, h: (b, 0, h)),
        out_shape=jax.ShapeDtypeStruct((B, S, H * dv), BF16),
        compiler_params=_params("parallel", "parallel"),
        name="mla_attention",
    )(q, k, v)


def _mix_epilogue(a_bf16, x_ref, w_ref, g_post_ref, g_ffn_ref, x_out_ref, hn_out_ref):
    m = _dot(a_bf16, w_ref[...])
    x1 = x_ref[...] + _rms(m, g_post_ref[...])
    x_out_ref[...] = x1
    hn_out_ref[...] = _rms(x1, g_ffn_ref[...]).astype(BF16)


def _mla_out_kernel(o_ref, x_ref, w_ref, g_post_ref, g_ffn_ref, x_out_ref, hn_out_ref):
    _mix_epilogue(o_ref[...], x_ref, w_ref, g_post_ref, g_ffn_ref, x_out_ref, hn_out_ref)


def _hgrn_out_kernel(of_ref, ob_ref, gate_ref, g_head_ref, x_ref, w_ref, g_post_ref, g_ffn_ref,
                     x_out_ref, hn_out_ref):
    heads = of_ref.shape[0]
    cols = []
    for h in range(heads):
        o = of_ref[h].astype(F32) + ob_ref[h].astype(F32)
        cols.append((_rms(o, g_head_ref[h]) * gate_ref[h]).astype(BF16))
    _mix_epilogue(jnp.concatenate(cols, axis=1), x_ref, w_ref, g_post_ref, g_ffn_ref,
                  x_out_ref, hn_out_ref)


def _mix_out_call(kern, lead_args, lead_specs, x2d, w_out, g_post, g_ffn, tm, name):
    T, D = x2d.shape
    row = pl.BlockSpec((tm, D), lambda i: (i, 0))
    return pl.pallas_call(
        kern,
        grid=(T // tm,),
        in_specs=lead_specs + [row, _const_spec(w_out.shape), _const_spec((1, D)), _const_spec((1, D))],
        out_specs=[row, row],
        out_shape=[jax.ShapeDtypeStruct((T, D), F32), jax.ShapeDtypeStruct((T, D), BF16)],
        compiler_params=_params("parallel"),
        name=name,
    )(*lead_args, x2d, w_out.astype(BF16), g_post[None, :], g_ffn[None, :])


def _gelu_tanh(x):
    c = 0.7978845608028654
    return 0.5 * x * (1.0 + jnp.tanh(c * (x + 0.044715 * (x * x * x))))


def _ffn_kernel(hn_ref, hprev_ref, hnext_ref, x_ref, w_in_ref, cw_ref, cb_ref, wo_ref,
                g_post_ref, g_next_ref, x_out_ref, *rest, tiles_per_seq, emit_next):
    if emit_next:
        hn_out_ref, act_ref = rest
    else:
        (act_ref,) = rest
    i = pl.program_id(0)
    tm = hn_ref.shape[0]
    hn = hn_ref[...]
    halo = jnp.concatenate([hprev_ref[...], hnext_ref[...]], axis=0)
    t_in_seq = i % tiles_per_seq
    seq_start = t_in_seq == 0
    seq_end = t_in_seq == tiles_per_seq - 1
    rows = lax.broadcasted_iota(jnp.int32, (tm, 1), 0)

    nc, _, cw_cols = act_ref.shape
    d_ff = nc * cw_cols

    def col_chunk(c):
        cols = slice(c * cw_cols, (c + 1) * cw_cols)
        wg = w_in_ref[:, cols]
        g = _dot(hn, wg)
        gh = _dot(halo, wg)
        val = _dot(hn, w_in_ref[:, d_ff + c * cw_cols:d_ff + (c + 1) * cw_cols])
        g_before = jnp.where(seq_start, 0.0, gh[V7X_BF16_ROWS - 1:V7X_BF16_ROWS, :])
        g_after = jnp.where(seq_end, 0.0, gh[V7X_BF16_ROWS:V7X_BF16_ROWS + 1, :])
        g_m1 = jnp.where(rows == 0, g_before, pltpu.roll(g, 1, 0))
        g_p1 = jnp.where(rows == tm - 1, g_after, pltpu.roll(g, tm - 1, 0))
        cw = cw_ref[:, cols]
        conv = g_m1 * cw[0:1, :] + g * cw[1:2, :] + g_p1 * cw[2:3, :] + cb_ref[:, cols]
        act_ref[c] = (_gelu_tanh(conv) * val).astype(BF16)

    for c in range(nc):
        col_chunk(c)
    f = _dot(act_ref[0], wo_ref[:cw_cols, :])
    for c in range(1, nc):
        f = f + _dot(act_ref[c], wo_ref[c * cw_cols:(c + 1) * cw_cols, :])
    x2 = x_ref[...] + _rms(f, g_post_ref[...])
    x_out_ref[...] = x2
    if emit_next:
        hn_out_ref[...] = _rms(x2, g_next_ref[...]).astype(BF16)


def _conv_ffn(hn, x2d, seq_len, w_in, conv_w, conv_b, w_out, g_post, g_next, *, emit_next, tm=512):
    T, D = x2d.shape
    d_ff = w_out.shape[0]
    cw = FFN_COL_CHUNK
    nc = d_ff // cw
    w_in16 = w_in.astype(BF16)
    wo = w_out.astype(BF16)
    cbt = conv_b[None, :]
    halo_rows = V7X_BF16_ROWS
    per_tile = tm // halo_rows
    n_halo = T // halo_rows
    row = pl.BlockSpec((tm, D), lambda i: (i, 0))
    out_specs = [row]
    out_shape = [jax.ShapeDtypeStruct((T, D), F32)]
    if emit_next:
        out_specs.append(row)
        out_shape.append(jax.ShapeDtypeStruct((T, D), BF16))
    kern = functools.partial(_ffn_kernel, tiles_per_seq=seq_len // tm, emit_next=emit_next)
    return pl.pallas_call(
        kern,
        grid=(T // tm,),
        in_specs=[
            row,
            pl.BlockSpec((halo_rows, D), lambda i: (jnp.maximum(i * per_tile - 1, 0), 0)),
            pl.BlockSpec((halo_rows, D), lambda i: (jnp.minimum((i + 1) * per_tile, n_halo - 1), 0)),
            row,
            _const_spec(w_in16.shape), _const_spec(conv_w.shape),
            _const_spec(cbt.shape), _const_spec(wo.shape),
            _const_spec((1, D)), _const_spec((1, D)),
        ],
        out_specs=out_specs,
        out_shape=out_shape,
        scratch_shapes=[pltpu.VMEM((nc, tm, cw), BF16)],
        compiler_params=_params("parallel"),
        name="conv_ffn",
    )(hn, hn, hn, x2d, w_in16, conv_w, cbt, wo, g_post[None, :], g_next[None, :])


def _hgrn_in_kernel(hn_ref, w_ref, o_ref, *, silu_scale, heads):
    res = _dot(hn_ref[...], w_ref[...])
    n_blocks = o_ref.shape[0]
    n_tiles = len(silu_scale) * heads // n_blocks
    n = pl.program_id(0)
    for cb in range(n_blocks):
        y = res[:, cb * V7X_LANES:(cb + 1) * V7X_LANES]
        per_tile = [silu_scale[(t * n_blocks + cb) // heads] for t in range(n_tiles)]
        out = y
        if any(s is not None for s in per_tile):
            act = _silu(y)
            for t, s in enumerate(per_tile):
                if s is not None:
                    out = jnp.where(n == t, act * s, out)
        o_ref[cb] = out


def _hgrn_in_proj(hn, w_in, *, heads, tm=1024, tn=2560):
    T, D = hn.shape
    N = w_in.shape[1]
    silu_scale = (float(HGRN_EXPAND) ** -0.5, None, None, None, 1.0)
    return pl.pallas_call(
        functools.partial(_hgrn_in_kernel, silu_scale=silu_scale, heads=heads),
        grid=(N // tn, T // tm),
        in_specs=[pl.BlockSpec((tm, D), lambda n, i: (i, 0)),
                  pl.BlockSpec((D, tn), lambda n, i: (0, n))],
        out_specs=pl.BlockSpec((tn // V7X_LANES, tm, V7X_LANES), lambda n, i: (n, i, 0)),
        out_shape=jax.ShapeDtypeStruct((N // V7X_LANES, T, V7X_LANES), F32),
        compiler_params=_params("parallel", "parallel"),
        name="hgrn_in_proj",
    )(hn, w_in.astype(BF16))


def _scan_gates(lbl_ref, q_ref, f_ref, d, h, layer_idx):
    depth = lbl_ref.shape[1]
    logits = [lbl_ref[d, i, h] for i in range(depth)]
    mx = functools.reduce(jnp.maximum, logits)
    es = [jnp.exp(l - mx) for l in logits]
    tot = functools.reduce(lambda a, b: a + b, es)
    picked = es[1:layer_idx + 1]
    lb = functools.reduce(lambda a, b: a + b, picked) / tot if picked else jnp.zeros_like(tot)
    one_m_lb = 1.0 - lb
    sig_f, sig_nf = _sigmoid_pair(f_ref[h])
    q = q_ref[h]
    return q, lb + one_m_lb * sig_f, one_m_lb * sig_nf


def _half_chunk_decay(f):
    nb = SCAN_CHUNK // V7X_SUBLANES
    out = []
    for half in range(2):
        bl = [f[V7X_SUBLANES * j:V7X_SUBLANES * (j + 1), :] for j in range(half * nb // 2, (half + 1) * nb // 2)]
        p = functools.reduce(lambda a, b: a * b, bl)
        for shift in (4, 2, 1):
            p = p * pltpu.roll(p, shift, 0)
        out.append(p)
    return out


def _scan_core(q, f, k, v, mask_ref, o_ref, st_ref, d, h, *, rev, fast):
    nb = SCAN_CHUNK // V7X_SUBLANES
    n_levels = mask_ref.shape[1] - 1

    sub = lax.broadcasted_iota(jnp.int32, (V7X_SUBLANES, V7X_LANES), 0)
    il = (V7X_SUBLANES - 1 - sub) if rev else sub

    def blocks(a):
        bl = [a[V7X_SUBLANES * j:V7X_SUBLANES * (j + 1), :] for j in range(nb)]
        return bl[::-1] if rev else bl

    def unblocks(bl):
        return jnp.concatenate(bl[::-1] if rev else bl, axis=0)

    def row(x, pos):
        u = (V7X_SUBLANES - 1 - pos) if rev else pos
        return jnp.broadcast_to(x[u:u + 1, :], x.shape)

    def at_prev(x):
        return pltpu.roll(x, (V7X_SUBLANES - 1) if rev else 1, 0)

    def at_next(x):
        return pltpu.roll(x, 1 if rev else (V7X_SUBLANES - 1), 0)

    qb, kb = blocks(q), blocks(k)
    hi1, hi2, hi4 = [(il & c) != 0 for c in (1, 2, 4)]

    def level_operands(F, G):
        qh = unblocks([a * b for a, b in zip(qb, F)]).astype(BF16)
        kh = unblocks([a * b for a, b in zip(kb, G)] if G is not None else kb).astype(BF16)
        return qh, kh

    last = V7X_SUBLANES - 1

    def next_prefix(F, lvl):
        if lvl == 0:
            return [x * jnp.where(hi1, at_prev(x), 1.0) for x in F]
        if lvl == 1:
            return [x * jnp.where(hi2, jnp.where(hi4, row(x, 5), row(x, 1)), 1.0) for x in F]
        if lvl == 2:
            return [x * jnp.where(hi4, row(x, 3), 1.0) for x in F]
        cb = 2 ** (lvl - 3)
        return [F[b] * row(F[(b & ~(2 * cb - 1)) + cb - 1], last) if b & cb else F[b]
                for b in range(nb)]

    def next_level(F, G, lvl):
        if lvl == 0:
            newG = [jnp.where(hi1, 1.0, at_next(x)) for x in F]
        elif lvl == 1:
            newG = [g * jnp.where(hi2, 1.0, jnp.where(hi4, row(x, 7), row(x, 3))) for x, g in zip(F, G)]
        elif lvl == 2:
            newG = [g * jnp.where(hi4, 1.0, row(x, 7)) for x, g in zip(F, G)]
        else:
            cb = 2 ** (lvl - 3)
            newG = [G[b] if b & cb else G[b] * row(F[(b & ~(2 * cb - 1)) + 2 * cb - 1], last)
                    for b in range(nb)]
        return next_prefix(F, lvl), newG

    F, G = blocks(f), None
    if fast:
        for lvl in range(n_levels - 1):
            F = next_prefix(F, lvl)
        half = nb // 2
        tot1, tot2 = row(F[half - 1], last), row(F[nb - 1], last)
        inv = [1.0 / x for x in F]
        inv_tot1 = 1.0 / tot1
        q_fac = [F[b] * inv_tot1 if b < half else F[b] for b in range(nb)]
        k_fac = [inv[b] * tot1 if b < half else inv[b] for b in range(nb)]
        qh = unblocks([a * b for a, b in zip(qb, q_fac)]).astype(BF16)
        kh = unblocks([a * b for a, b in zip(kb, k_fac)]).astype(BF16)
        s_acc = jnp.where(mask_ref[d, n_levels] != 0.0, _dot_nt(qh, kh), 0.0)
        tot = tot1 * tot2
        G = [inv[b] * (tot if b < half else tot2) for b in range(nb)]
        F = [F[b] if b < half else F[b] * tot1 for b in range(nb)]
    else:
        s_acc = None
        for lvl in range(n_levels):
            qh, kh = level_operands(F, G)
            term = _dot_nt(qh, kh) * mask_ref[d, lvl]
            s_acc = term if s_acc is None else s_acc + term
            F, G = next_level(F, G, lvl)

    q_dec, k_dec = level_operands(F, G)
    st = st_ref[d, h]
    v16 = v.astype(BF16)
    diag = jnp.sum(q * k, axis=-1, keepdims=True)
    o = _dot(q_dec, st.astype(BF16)) + _dot(s_acc.astype(BF16), v16) + diag * v
    o_ref[h] = o.astype(o_ref.dtype)
    f_total = row(F[nb - 1], last)[0:1, :]
    decay_rows = jnp.broadcast_to(f_total, (st.shape[1], st.shape[0])).T
    st_ref[d, h] = st * decay_rows + _dot_tn(k_dec, v16)


def _scan_level_masks():
    n = SCAN_CHUNK
    pos = np.arange(n)
    n_levels = n.bit_length() - 1
    out = np.zeros((2, n_levels + 1, n, n), np.float32)
    for d in range(2):
        idx = (n - 1 - pos) if d == 1 else pos
        ti, si = idx[:, None], idx[None, :]
        lvl, c = 0, 1
        while c < n:
            out[d, lvl] = ((ti & c) != 0) & ((si & c) == 0) & ((ti // (2 * c)) == (si // (2 * c)))
            lvl, c = lvl + 1, 2 * c
        out[d, n_levels] = out[d, :n_levels].sum(axis=0)
    return out


def _hgrn_scan_kernel(lbl_ref, mask_ref, qf_ref, ff_ref, vf_ref, qb_ref, fb_ref, vb_ref, of_ref,
                      ob_ref, st_ref, gate_ref, *, layer_idx):
    @pl.when(pl.program_id(1) == 0)
    def _():
        st_ref[...] = jnp.zeros_like(st_ref)

    heads = qf_ref.shape[0]
    dirs = ((qf_ref, ff_ref, vf_ref, of_ref), (qb_ref, fb_ref, vb_ref, ob_ref))

    def gate_head(h, weakest):
        for d, (q_ref, f_ref, _, _) in enumerate(dirs):
            q, f, k = _scan_gates(lbl_ref, q_ref, f_ref, d, h, layer_idx)
            gate_ref[0, d, h] = q
            gate_ref[1, d, h] = f
            gate_ref[2, d, h] = k
            for p in _half_chunk_decay(f):
                weakest = jnp.minimum(weakest, p)
        return weakest

    weakest = lax.fori_loop(0, heads, gate_head,
                            jnp.ones((V7X_SUBLANES, V7X_LANES), F32), unroll=heads)
    fast_ok = jnp.min(weakest) >= SCAN_FAST_MIN_HALF_DECAY

    def one(h, d, fast):
        _, _, v_ref, o_ref = dirs[d]
        _scan_core(gate_ref[0, d, h], gate_ref[1, d, h], gate_ref[2, d, h], v_ref[h], mask_ref,
                   o_ref, st_ref, d, h, rev=d == 1, fast=fast)

    @pl.when(fast_ok)
    def _():
        for h in range(heads):
            for d in range(2):
                one(h, d, True)

    @pl.when(jnp.logical_not(fast_ok))
    def _():
        def head(h, carry):
            for d in range(2):
                one(h, d, False)
            return carry
        lax.fori_loop(0, heads, head, 0, unroll=2)


def _hgrn_scan(proj, lb_logits, *, batch, seq_len, layer_idx, heads):
    T = proj.shape[1]
    hd = proj.shape[2]
    nchunk = seq_len // SCAN_CHUNK
    p5 = proj.reshape(5, heads, batch, seq_len, hd)
    depth = lb_logits.shape[1]
    lbl = lb_logits.reshape(2, depth, heads, 1, hd)
    masks = jnp.asarray(_scan_level_masks())

    def sec(section, backward):
        if backward:
            return pl.BlockSpec((None, heads, None, SCAN_CHUNK, hd),
                                lambda b, j: (section, 0, b, nchunk - 1 - j, 0))
        return pl.BlockSpec((None, heads, None, SCAN_CHUNK, hd), lambda b, j: (section, 0, b, j, 0))

    out_f = pl.BlockSpec((heads, None, SCAN_CHUNK, hd), lambda b, j: (0, b, j, 0))
    out_b = pl.BlockSpec((heads, None, SCAN_CHUNK, hd), lambda b, j: (0, b, nchunk - 1 - j, 0))
    o_sds = jax.ShapeDtypeStruct((heads, batch, seq_len, hd), BF16)
    of, ob = pl.pallas_call(
        functools.partial(_hgrn_scan_kernel, layer_idx=layer_idx),
        grid=(batch, nchunk),
        in_specs=[_const_spec(lbl.shape), _const_spec(masks.shape),
                  sec(0, False), sec(1, False), sec(3, False),
                  sec(0, True), sec(2, True), sec(3, True)],
        out_specs=[out_f, out_b],
        out_shape=[o_sds, o_sds],
        scratch_shapes=[pltpu.VMEM((2, heads, hd, hd), F32),
                        pltpu.VMEM((3, 2, heads, SCAN_CHUNK, hd), F32)],
        compiler_params=_params("arbitrary", "arbitrary"),
        name="hgrn_scan",
    )(lbl, masks, p5, p5, p5, p5, p5, p5)
    return of.reshape(heads, T, hd), ob.reshape(heads, T, hd)


def kernel(x, positions, pre_mix_norm, post_mix_norm, pre_ffn_norm, post_ffn_norm, mla_w_in, mla_q_norm, mla_w_q_up, mla_kv_norm, mla_w_kv_up, mla_w_out, hgrn_w_in, hgrn_lb_logits, hgrn_out_norm, hgrn_w_out, ffn_w_in, ffn_conv_w, ffn_conv_b, ffn_w_out):
    B, S, D = x.shape
    T = B * S
    depth = pre_mix_norm.shape[0]
    n_mixers = 2
    hgrn_heads = D // HGRN_EXPAND
    x2d = x.reshape(T, D)
    hn = None
    tm_out = 512
    for l in range(depth):
        j = l // n_mixers
        if l % n_mixers == 0:
            q, k, v = _mla_proj(x2d.reshape(B, S, D), positions, pre_mix_norm[l], mla_w_in[j],
                                mla_q_norm[j], mla_w_q_up[j], mla_kv_norm[j], mla_w_kv_up[j])
            o = _attention(q, k, v).reshape(T, -1)
            x2d, hn = _mix_out_call(
                _mla_out_kernel, [o], [pl.BlockSpec((tm_out, o.shape[1]), lambda i: (i, 0))],
                x2d, mla_w_out[j], post_mix_norm[l], pre_ffn_norm[l], tm_out, "mla_out")
        else:
            proj = _hgrn_in_proj(hn, hgrn_w_in[j], heads=hgrn_heads)
            of, ob = _hgrn_scan(proj, hgrn_lb_logits, batch=B, seq_len=S, layer_idx=l,
                                heads=hgrn_heads)
            hd = proj.shape[2]
            head_rows = pl.BlockSpec((hgrn_heads, tm_out, hd), lambda i: (0, i, 0))
            gate_rows = pl.BlockSpec((hgrn_heads, tm_out, hd), lambda i: (4, i, 0))
            x2d, hn = _mix_out_call(
                _hgrn_out_kernel,
                [of, ob, proj, hgrn_out_norm[j].reshape(hgrn_heads, 1, hd)],
                [head_rows, head_rows, gate_rows, _const_spec((hgrn_heads, 1, hd))],
                x2d, hgrn_w_out[j], post_mix_norm[l], pre_ffn_norm[l], tm_out, "hgrn_out")
        last = l == depth - 1
        g_next = pre_mix_norm[l] if last else pre_mix_norm[l + 1]
        outs = _conv_ffn(hn, x2d, S, ffn_w_in[l], ffn_conv_w[l], ffn_conv_b[l], ffn_w_out[l],
                         post_ffn_norm[l], g_next, emit_next=not last)
        if last:
            (x2d,) = outs
        else:
            x2d, hn = outs
    return x2d.reshape(B, S, D)
```

```python
import functools

import jax
import jax.numpy as jnp
import numpy as np
from jax import lax
from jax.experimental import pallas as pl
from jax.experimental.pallas import tpu as pltpu

F32 = jnp.float32
BF16 = jnp.bfloat16

EPS = 1e-6
LOG2_E = 1.4426950408889634
ROPE_THETA = 10000.0
MLA_NOPE = 128
MLA_ROPE = 64
MLA_V = 128
HGRN_EXPAND = 128
SCAN_CHUNK = 64
SCAN_FAST_MIN_HALF_DECAY = 2.0 ** -100
FFN_COL_CHUNK = 256

V7X_LANES = 128
V7X_SUBLANES = 8
V7X_BF16_ROWS = 16
V7X_VMEM_LIMIT = 56 * 1024 * 1024


def _rms(x, gain):
    ms = jnp.mean(x * x, axis=-1, keepdims=True)
    return x * lax.rsqrt(ms + EPS) * gain


def _sigmoid_pair(x):
    e = jnp.exp(-jnp.abs(x))
    r = 1.0 / (1.0 + e)
    er = e * r
    pos = x >= 0
    return jnp.where(pos, r, er), jnp.where(pos, er, r)


def _silu(x):
    return x / (1.0 + jnp.exp(-x))


def _dot(a, b):
    return jnp.dot(a, b, preferred_element_type=F32)


def _dot_nt(a, b):
    return lax.dot_general(a, b, (((1,), (1,)), ((), ())), preferred_element_type=F32)


def _dot_tn(a, b):
    return lax.dot_general(a, b, (((0,), (0,)), ((), ())), preferred_element_type=F32)


def _params(*sem):
    return pltpu.CompilerParams(dimension_semantics=sem, vmem_limit_bytes=V7X_VMEM_LIMIT)


def _const_spec(shape):
    nd = len(shape)
    return pl.BlockSpec(shape, lambda *_: (0,) * nd)


def _mla_proj_kernel(x_ref, pos_ref, g_pre_ref, w_in_ref, g_q_ref, w_q_ref, g_kv_ref,
                     w_kv_ref, invf_ref, sgn_ref, q_ref, k_ref, v_ref, *, q_lora, kv_lora, heads):
    hn = _rms(x_ref[...], g_pre_ref[...]).astype(BF16)
    proj = _dot(hn, w_in_ref[...])
    c_q = proj[:, :q_lora]
    c_kv = proj[:, q_lora:q_lora + kv_lora]
    kr = proj[:, q_lora + kv_lora:q_lora + kv_lora + MLA_ROPE]
    kr_sw = proj[:, q_lora + kv_lora + MLA_ROPE:]

    ang = pos_ref[...].astype(F32) * invf_ref[...]
    cos2 = jnp.cos(ang)
    sin2 = jnp.sin(ang) * sgn_ref[...]

    q_all = _dot(_rms(c_q, g_q_ref[...]).astype(BF16), w_q_ref[...])
    kv = _dot(_rms(c_kv, g_kv_ref[...]).astype(BF16), w_kv_ref[...])

    n_nope = heads * MLA_NOPE
    n_rope = heads * MLA_ROPE
    k_rot = (kr * cos2[:, :MLA_ROPE] + kr_sw * sin2[:, :MLA_ROPE]).astype(BF16)
    for pair in range(heads // 2):
        lo = n_nope + pair * V7X_LANES
        q_rot = (q_all[:, lo:lo + V7X_LANES] * cos2
                 + q_all[:, lo + n_rope:lo + n_rope + V7X_LANES] * sin2).astype(BF16)
        for sub in range(2):
            h = 2 * pair + sub
            q_ref[h, :, MLA_NOPE:] = q_rot[:, sub * MLA_ROPE:(sub + 1) * MLA_ROPE]
    for h in range(heads):
        q_ref[h, :, :MLA_NOPE] = q_all[:, h * MLA_NOPE:(h + 1) * MLA_NOPE].astype(BF16)
        k_ref[h, :, :MLA_NOPE] = kv[:, h * MLA_NOPE:(h + 1) * MLA_NOPE].astype(BF16)
        k_ref[h, :, MLA_NOPE:] = k_rot
        v_ref[h, :, :MLA_V] = kv[:, n_nope + h * MLA_V:n_nope + (h + 1) * MLA_V].astype(BF16)
        v_ref[h, :, MLA_V:] = jnp.ones((v_ref.shape[1], MLA_V), BF16)


def _mla_proj(x, positions, g_pre, w_in, g_q, w_q_up, g_kv, w_kv_up, *, tm=512):
    B, S, D = x.shape
    q_lora = g_q.shape[0]
    kv_lora = g_kv.shape[0]
    heads = w_q_up.shape[1] // (MLA_NOPE + MLA_ROPE)
    half = MLA_ROPE // 2
    qk_dim = MLA_NOPE + MLA_ROPE

    kr_lo = q_lora + kv_lora
    w_in_x = jnp.concatenate(
        [w_in, w_in[:, kr_lo + half:kr_lo + MLA_ROPE], w_in[:, kr_lo:kr_lo + half]], axis=1).astype(BF16)
    wq3 = w_q_up.reshape(q_lora, heads, qk_dim)
    rope = wq3[:, :, MLA_NOPE:]
    rope_sw = jnp.concatenate([rope[:, :, half:], rope[:, :, :half]], axis=-1)
    w_q_x = jnp.concatenate(
        [wq3[:, :, :MLA_NOPE].reshape(q_lora, -1), rope.reshape(q_lora, -1),
         rope_sw.reshape(q_lora, -1)], axis=1).astype(BF16)
    wkv3 = w_kv_up.reshape(kv_lora, heads, MLA_NOPE + MLA_V)
    w_kv_x = jnp.concatenate(
        [wkv3[:, :, :MLA_NOPE].reshape(kv_lora, -1), wkv3[:, :, MLA_NOPE:].reshape(kv_lora, -1)],
        axis=1).astype(BF16)

    inv_freq = 1.0 / (ROPE_THETA ** (jnp.arange(0, MLA_ROPE, 2, dtype=F32) / MLA_ROPE))
    invf = jnp.tile(inv_freq, V7X_LANES // half)[None, :]
    sgn = jnp.tile(jnp.concatenate([-jnp.ones((half,), F32), jnp.ones((half,), F32)]),
                   V7X_LANES // MLA_ROPE)[None, :]
    scale = float(qk_dim) ** -0.5 * LOG2_E

    kern = functools.partial(_mla_proj_kernel, q_lora=q_lora, kv_lora=kv_lora, heads=heads)
    head_out = lambda width: pl.BlockSpec((None, heads, tm, width), lambda b, i: (b, 0, i, 0))
    return pl.pallas_call(
        kern,
        grid=(B, S // tm),
        in_specs=[
            pl.BlockSpec((None, tm, D), lambda b, i: (b, i, 0)),
            pl.BlockSpec((None, tm, 1), lambda b, i: (b, i, 0)),
            _const_spec((1, D)),
            _const_spec(w_in_x.shape),
            _const_spec((1, q_lora)),
            _const_spec(w_q_x.shape),
            _const_spec((1, kv_lora)),
            _const_spec(w_kv_x.shape),
            _const_spec((1, V7X_LANES)),
            _const_spec((1, V7X_LANES)),
        ],
        out_specs=[head_out(qk_dim), head_out(qk_dim), head_out(2 * MLA_V)],
        out_shape=[
            jax.ShapeDtypeStruct((B, heads, S, qk_dim), BF16),
            jax.ShapeDtypeStruct((B, heads, S, qk_dim), BF16),
            jax.ShapeDtypeStruct((B, heads, S, 2 * MLA_V), BF16),
        ],
        compiler_params=_params("parallel", "parallel"),
        name="mla_proj",
    )(x, positions.reshape(B, S, 1), g_pre[None, :], w_in_x, (g_q * scale)[None, :], w_q_x,
      g_kv[None, :], w_kv_x, invf, sgn)


def _attn_kernel(q_ref, k_ref, v_ref, o_ref, *, tq):
    def q_tile(i, carry):
        r0 = pl.multiple_of(i * tq, tq)
        s = _dot_nt(q_ref[pl.ds(r0, tq), :], k_ref[...])
        p = jnp.exp2(s - jnp.max(s, axis=-1, keepdims=True))
        o = _dot(p.astype(BF16), v_ref[...])
        dv = o_ref.shape[1]
        o_ref[pl.ds(r0, tq), :] = (o[:, :dv] / o[:, dv:]).astype(o_ref.dtype)
        return carry

    lax.fori_loop(0, q_ref.shape[0] // tq, q_tile, 0, unroll=True)


def _attention(q, k, v, *, tq=256):
    B, H, S, dk = q.shape
    dv = v.shape[-1] // 2
    per_head = lambda width: pl.BlockSpec((None, None, S, width), lambda b, h: (b, h, 0, 0))
    return pl.pallas_call(
        functools.partial(_attn_kernel, tq=tq),
        grid=(B, H),
        in_specs=[per_head(dk), per_head(dk), per_head(2 * dv)],
        out_specs=pl.BlockSpec((None, S, dv), lambda b, h: (b, 0, h)),
        out_shape=jax.ShapeDtypeStruct((B, S, H * dv), BF16),
        compiler_params=_params("parallel", "parallel"),
        name="mla_attention",
    )(q, k, v)


def _mix_epilogue(a_bf16, x_ref, w_ref, g_post_ref, g_ffn_ref, x_out_ref, hn_out_ref):
    m = _dot(a_bf16, w_ref[...])
    x1 = x_ref[...] + _rms(m, g_post_ref[...])
    x_out_ref[...] = x1
    hn_out_ref[...] = _rms(x1, g_ffn_ref[...]).astype(BF16)


def _mla_out_kernel(o_ref, x_ref, w_ref, g_post_ref, g_ffn_ref, x_out_ref, hn_out_ref):
    _mix_epilogue(o_ref[...], x_ref, w_ref, g_post_ref, g_ffn_ref, x_out_ref, hn_out_ref)


def _hgrn_out_kernel(of_ref, ob_ref, gate_ref, g_head_ref, x_ref, w_ref, g_post_ref, g_ffn_ref,
                     x_out_ref, hn_out_ref):
    heads = of_ref.shape[0]
    cols = []
    for h in range(heads):
        o = of_ref[h].astype(F32) + ob_ref[h].astype(F32)
        cols.append((_rms(o, g_head_ref[h]) * gate_ref[h]).astype(BF16))
    _mix_epilogue(jnp.concatenate(cols, axis=1), x_ref, w_ref, g_post_ref, g_ffn_ref,
                  x_out_ref, hn_out_ref)


def _mix_out_call(kern, lead_args, lead_specs, x2d, w_out, g_post, g_ffn, tm, name):
    T, D = x2d.shape
    row = pl.BlockSpec((tm, D), lambda i: (i, 0))
    return pl.pallas_call(
        kern,
        grid=(T // tm,),
        in_specs=lead_specs + [row, _const_spec(w_out.shape), _const_spec((1, D)), _const_spec((1, D))],
        out_specs=[row, row],
        out_shape=[jax.ShapeDtypeStruct((T, D), F32), jax.ShapeDtypeStruct((T, D), BF16)],
        compiler_params=_params("parallel"),
        name=name,
    )(*lead_args, x2d, w_out.astype(BF16), g_post[None, :], g_ffn[None, :])


def _gelu_tanh(x):
    c = 0.7978845608028654
    return 0.5 * x * (1.0 + jnp.tanh(c * (x + 0.044715 * (x * x * x))))


def _ffn_kernel(hn_ref, hprev_ref, hnext_ref, x_ref, w_in_ref, cw_ref, cb_ref, wo_ref,
                g_post_ref, g_next_ref, x_out_ref, *rest, tiles_per_seq, emit_next):
    if emit_next:
        hn_out_ref, act_ref = rest
    else:
        (act_ref,) = rest
    i = pl.program_id(0)
    tm = hn_ref.shape[0]
    hn = hn_ref[...]
    halo = jnp.concatenate([hprev_ref[...], hnext_ref[...]], axis=0)
    t_in_seq = i % tiles_per_seq
    seq_start = t_in_seq == 0
    seq_end = t_in_seq == tiles_per_seq - 1
    rows = lax.broadcasted_iota(jnp.int32, (tm, 1), 0)

    nc, _, cw_cols = act_ref.shape
    d_ff = nc * cw_cols

    def col_chunk(c):
        cols = slice(c * cw_cols, (c + 1) * cw_cols)
        wg = w_in_ref[:, cols]
        g = _dot(hn, wg)
        gh = _dot(halo, wg)
        val = _dot(hn, w_in_ref[:, d_ff + c * cw_cols:d_ff + (c + 1) * cw_cols])
        g_before = jnp.where(seq_start, 0.0, gh[V7X_BF16_ROWS - 1:V7X_BF16_ROWS, :])
        g_after = jnp.where(seq_end, 0.0, gh[V7X_BF16_ROWS:V7X_BF16_ROWS + 1, :])
        g_m1 = jnp.where(rows == 0, g_before, pltpu.roll(g, 1, 0))
        g_p1 = jnp.where(rows == tm - 1, g_after, pltpu.roll(g, tm - 1, 0))
        cw = cw_ref[:, cols]
        conv = g_m1 * cw[0:1, :] + g * cw[1:2, :] + g_p1 * cw[2:3, :] + cb_ref[:, cols]
        act_ref[c] = (_gelu_tanh(conv) * val).astype(BF16)

    for c in range(nc):
        col_chunk(c)
    f = _dot(act_ref[0], wo_ref[:cw_cols, :])
    for c in range(1, nc):
        f = f + _dot(act_ref[c], wo_ref[c * cw_cols:(c + 1) * cw_cols, :])
    x2 = x_ref[...] + _rms(f, g_post_ref[...])
    x_out_ref[...] = x2
    if emit_next:
        hn_out_ref[...] = _rms(x2, g_next_ref[...]).astype(BF16)


def _conv_ffn(hn, x2d, seq_len, w_in, conv_w, conv_b, w_out, g_post, g_next, *, emit_next, tm=512):
    T, D = x2d.shape
    d_ff = w_out.shape[0]
    cw = FFN_COL_CHUNK
    nc = d_ff // cw
    w_in16 = w_in.astype(BF16)
    wo = w_out.astype(BF16)
    cbt = conv_b[None, :]
    halo_rows = V7X_BF16_ROWS
    per_tile = tm // halo_rows
    n_halo = T // halo_rows
    row = pl.BlockSpec((tm, D), lambda i: (i, 0))
    out_specs = [row]
    out_shape = [jax.ShapeDtypeStruct((T, D), F32)]
    if emit_next:
        out_specs.append(row)
        out_shape.append(jax.ShapeDtypeStruct((T, D), BF16))
    kern = functools.partial(_ffn_kernel, tiles_per_seq=seq_len // tm, emit_next=emit_next)
    return pl.pallas_call(
        kern,
        grid=(T // tm,),
        in_specs=[
            row,
            pl.BlockSpec((halo_rows, D), lambda i: (jnp.maximum(i * per_tile - 1, 0), 0)),
            pl.BlockSpec((halo_rows, D), lambda i: (jnp.minimum((i + 1) * per_tile, n_halo - 1), 0)),
            row,
            _const_spec(w_in16.shape), _const_spec(conv_w.shape),
            _const_spec(cbt.shape), _const_spec(wo.shape),
            _const_spec((1, D)), _const_spec((1, D)),
        ],
        out_specs=out_specs,
        out_shape=out_shape,
        scratch_shapes=[pltpu.VMEM((nc, tm, cw), BF16)],
        compiler_params=_params("parallel"),
        name="conv_ffn",
    )(hn, hn, hn, x2d, w_in16, conv_w, cbt, wo, g_post[None, :], g_next[None, :])


def _hgrn_in_kernel(hn_ref, w_ref, o_ref, *, silu_scale, heads):
    res = _dot(hn_ref[...], w_ref[...])
    n_blocks = o_ref.shape[0]
    n_tiles = len(silu_scale) * heads // n_blocks
    n = pl.program_id(0)
    for cb in range(n_blocks):
        y = res[:, cb * V7X_LANES:(cb + 1) * V7X_LANES]
        per_tile = [silu_scale[(t * n_blocks + cb) // heads] for t in range(n_tiles)]
        out = y
        if any(s is not None for s in per_tile):
            act = _silu(y)
            for t, s in enumerate(per_tile):
                if s is not None:
                    out = jnp.where(n == t, act * s, out)
        o_ref[cb] = out


def _hgrn_in_proj(hn, w_in, *, heads, tm=1024, tn=2560):
    T, D = hn.shape
    N = w_in.shape[1]
    silu_scale = (float(HGRN_EXPAND) ** -0.5, None, None, None, 1.0)
    return pl.pallas_call(
        functools.partial(_hgrn_in_kernel, silu_scale=silu_scale, heads=heads),
        grid=(N // tn, T // tm),
        in_specs=[pl.BlockSpec((tm, D), lambda n, i: (i, 0)),
                  pl.BlockSpec((D, tn), lambda n, i: (0, n))],
        out_specs=pl.BlockSpec((tn // V7X_LANES, tm, V7X_LANES), lambda n, i: (n, i, 0)),
        out_shape=jax.ShapeDtypeStruct((N // V7X_LANES, T, V7X_LANES), F32),
        compiler_params=_params("parallel", "parallel"),
        name="hgrn_in_proj",
    )(hn, w_in.astype(BF16))


def _scan_gates(lbl_ref, q_ref, f_ref, d, h, layer_idx):
    depth = lbl_ref.shape[1]
    logits = [lbl_ref[d, i, h] for i in range(depth)]
    mx = functools.reduce(jnp.maximum, logits)
    es = [jnp.exp(l - mx) for l in logits]
    tot = functools.reduce(lambda a, b: a + b, es)
    picked = es[1:layer_idx + 1]
    lb = functools.reduce(lambda a, b: a + b, picked) / tot if picked else jnp.zeros_like(tot)
    one_m_lb = 1.0 - lb
    sig_f, sig_nf = _sigmoid_pair(f_ref[h])
    q = q_ref[h]
    return q, lb + one_m_lb * sig_f, one_m_lb * sig_nf


def _half_chunk_decay(f):
    nb = SCAN_CHUNK // V7X_SUBLANES
    out = []
    for half in range(2):
        bl = [f[V7X_SUBLANES * j:V7X_SUBLANES * (j + 1), :] for j in range(half * nb // 2, (half + 1) * nb // 2)]
        p = functools.reduce(lambda a, b: a * b, bl)
        for shift in (4, 2, 1):
            p = p * pltpu.roll(p, shift, 0)
        out.append(p)
    return out


def _scan_core(q, f, k, v, mask_ref, o_ref, st_ref, d, h, *, rev, fast):
    nb = SCAN_CHUNK // V7X_SUBLANES
    n_levels = mask_ref.shape[1] - 1

    sub = lax.broadcasted_iota(jnp.int32, (V7X_SUBLANES, V7X_LANES), 0)
    il = (V7X_SUBLANES - 1 - sub) if rev else sub

    def blocks(a):
        bl = [a[V7X_SUBLANES * j:V7X_SUBLANES * (j + 1), :] for j in range(nb)]
        return bl[::-1] if rev else bl

    def unblocks(bl):
        return jnp.concatenate(bl[::-1] if rev else bl, axis=0)

    def row(x, pos):
        u = (V7X_SUBLANES - 1 - pos) if rev else pos
        return jnp.broadcast_to(x[u:u + 1, :], x.shape)

    def at_prev(x):
        return pltpu.roll(x, (V7X_SUBLANES - 1) if rev else 1, 0)

    def at_next(x):
        return pltpu.roll(x, 1 if rev else (V7X_SUBLANES - 1), 0)

    qb, kb = blocks(q), blocks(k)
    hi1, hi2, hi4 = [(il & c) != 0 for c in (1, 2, 4)]

    def level_operands(F, G):
        qh = unblocks([a * b for a, b in zip(qb, F)]).astype(BF16)
        kh = unblocks([a * b for a, b in zip(kb, G)] if G is not None else kb).astype(BF16)
        return qh, kh

    last = V7X_SUBLANES - 1

    def next_prefix(F, lvl):
        if lvl == 0:
            return [x * jnp.where(hi1, at_prev(x), 1.0) for x in F]
        if lvl == 1:
            return [x * jnp.where(hi2, jnp.where(hi4, row(x, 5), row(x, 1)), 1.0) for x in F]
        if lvl == 2:
            return [x * jnp.where(hi4, row(x, 3), 1.0) for x in F]
        cb = 2 ** (lvl - 3)
        return [F[b] * row(F[(b & ~(2 * cb - 1)) + cb - 1], last) if b & cb else F[b]
                for b in range(nb)]

    def next_level(F, G, lvl):
        if lvl == 0:
            newG = [jnp.where(hi1, 1.0, at_next(x)) for x in F]
        elif lvl == 1:
            newG = [g * jnp.where(hi2, 1.0, jnp.where(hi4, row(x, 7), row(x, 3))) for x, g in zip(F, G)]
        elif lvl == 2:
            newG = [g * jnp.where(hi4, 1.0, row(x, 7)) for x, g in zip(F, G)]
        else:
            cb = 2 ** (lvl - 3)
            newG = [G[b] if b & cb else G[b] * row(F[(b & ~(2 * cb - 1)) + 2 * cb - 1], last)
                    for b in range(nb)]
        return next_prefix(F, lvl), newG

    F, G = blocks(f), None
    if fast:
        for lvl in range(n_levels - 1):
            F = next_prefix(F, lvl)
        half = nb // 2
        tot1, tot2 = row(F[half - 1], last), row(F[nb - 1], last)
        inv = [1.0 / x for x in F]
        inv_tot1 = 1.0 / tot1
        q_fac = [F[b] * inv_tot1 if b < half else F[b] for b in range(nb)]
        k_fac = [inv[b] * tot1 if b < half else inv[b] for b in range(nb)]
        qh = unblocks([a * b for a, b in zip(qb, q_fac)]).astype(BF16)
        kh = unblocks([a * b for a, b in zip(kb, k_fac)]).astype(BF16)
        s_acc = jnp.where(mask_ref[d, n_levels] != 0.0, _dot_nt(qh, kh), 0.0)
        tot = tot1 * tot2
        G = [inv[b] * (tot if b < half else tot2) for b in range(nb)]
        F = [F[b] if b < half else F[b] * tot1 for b in range(nb)]
    else:
        s_acc = None
        for lvl in range(n_levels):
            qh, kh = level_operands(F, G)
            term = _dot_nt(qh, kh) * mask_ref[d, lvl]
            s_acc = term if s_acc is None else s_acc + term
            F, G = next_level(F, G, lvl)

    q_dec, k_dec = level_operands(F, G)
    st = st_ref[d, h]
    v16 = v.astype(BF16)
    o = _dot(q_dec, st.astype(BF16)) + _dot(s_acc.astype(BF16), v16)
    if not fast:
        o = o + jnp.sum(q * k, axis=-1, keepdims=True) * v
    o_ref[h] = o.astype(o_ref.dtype)
    f_total = row(F[nb - 1], last)[0:1, :]
    decay_rows = jnp.broadcast_to(f_total, (st.shape[1], st.shape[0])).T
    st_ref[d, h] = st * decay_rows + _dot_tn(k_dec, v16)


def _scan_level_masks():
    n = SCAN_CHUNK
    pos = np.arange(n)
    n_levels = n.bit_length() - 1
    out = np.zeros((2, n_levels + 1, n, n), np.float32)
    for d in range(2):
        idx = (n - 1 - pos) if d == 1 else pos
        ti, si = idx[:, None], idx[None, :]
        lvl, c = 0, 1
        while c < n:
            out[d, lvl] = ((ti & c) != 0) & ((si & c) == 0) & ((ti // (2 * c)) == (si // (2 * c)))
            lvl, c = lvl + 1, 2 * c
        out[d, n_levels] = out[d, :n_levels].sum(axis=0) + np.eye(n, dtype=np.float32)
    return out


def _hgrn_scan_kernel(lbl_ref, mask_ref, qf_ref, ff_ref, vf_ref, qb_ref, fb_ref, vb_ref, of_ref,
                      ob_ref, st_ref, gate_ref, *, layer_idx):
    @pl.when(pl.program_id(1) == 0)
    def _():
        st_ref[...] = jnp.zeros_like(st_ref)

    heads = qf_ref.shape[0]
    dirs = ((qf_ref, ff_ref, vf_ref, of_ref), (qb_ref, fb_ref, vb_ref, ob_ref))

    def gate_head(h, weakest):
        for d, (q_ref, f_ref, _, _) in enumerate(dirs):
            q, f, k = _scan_gates(lbl_ref, q_ref, f_ref, d, h, layer_idx)
            gate_ref[0, d, h] = q
            gate_ref[1, d, h] = f
            gate_ref[2, d, h] = k
            for p in _half_chunk_decay(f):
                weakest = jnp.minimum(weakest, p)
        return weakest

    weakest = lax.fori_loop(0, heads, gate_head,
                            jnp.ones((V7X_SUBLANES, V7X_LANES), F32), unroll=heads)
    fast_ok = jnp.min(weakest) >= SCAN_FAST_MIN_HALF_DECAY

    def one(h, d, fast):
        _, _, v_ref, o_ref = dirs[d]
        _scan_core(gate_ref[0, d, h], gate_ref[1, d, h], gate_ref[2, d, h], v_ref[h], mask_ref,
                   o_ref, st_ref, d, h, rev=d == 1, fast=fast)

    @pl.when(fast_ok)
    def _():
        for h in range(heads):
            for d in range(2):
                one(h, d, True)

    @pl.when(jnp.logical_not(fast_ok))
    def _():
        def head(h, carry):
            for d in range(2):
                one(h, d, False)
            return carry
        lax.fori_loop(0, heads, head, 0, unroll=2)


def _hgrn_scan(proj, lb_logits, *, batch, seq_len, layer_idx, heads):
    T = proj.shape[1]
    hd = proj.shape[2]
    nchunk = seq_len // SCAN_CHUNK
    p5 = proj.reshape(5, heads, batch, seq_len, hd)
    depth = lb_logits.shape[1]
    lbl = lb_logits.reshape(2, depth, heads, 1, hd)
    masks = jnp.asarray(_scan_level_masks())

    def sec(section, backward):
        if backward:
            return pl.BlockSpec((None, heads, None, SCAN_CHUNK, hd),
                                lambda b, j: (section, 0, b, nchunk - 1 - j, 0))
        return pl.BlockSpec((None, heads, None, SCAN_CHUNK, hd), lambda b, j: (section, 0, b, j, 0))

    out_f = pl.BlockSpec((heads, None, SCAN_CHUNK, hd), lambda b, j: (0, b, j, 0))
    out_b = pl.BlockSpec((heads, None, SCAN_CHUNK, hd), lambda b, j: (0, b, nchunk - 1 - j, 0))
    o_sds = jax.ShapeDtypeStruct((heads, batch, seq_len, hd), BF16)
    of, ob = pl.pallas_call(
        functools.partial(_hgrn_scan_kernel, layer_idx=layer_idx),
        grid=(batch, nchunk),
        in_specs=[_const_spec(lbl.shape), _const_spec(masks.shape),
                  sec(0, False), sec(1, False), sec(3, False),
                  sec(0, True), sec(2, True), sec(3, True)],
        out_specs=[out_f, out_b],
        out_shape=[o_sds, o_sds],
        scratch_shapes=[pltpu.VMEM((2, heads, hd, hd), F32),
                        pltpu.VMEM((3, 2, heads, SCAN_CHUNK, hd), F32)],
        compiler_params=_params("arbitrary", "arbitrary"),
        name="hgrn_scan",
    )(lbl, masks, p5, p5, p5, p5, p5, p5)
    return of.reshape(heads, T, hd), ob.reshape(heads, T, hd)


def kernel(x, positions, pre_mix_norm, post_mix_norm, pre_ffn_norm, post_ffn_norm, mla_w_in, mla_q_norm, mla_w_q_up, mla_kv_norm, mla_w_kv_up, mla_w_out, hgrn_w_in, hgrn_lb_logits, hgrn_out_norm, hgrn_w_out, ffn_w_in, ffn_conv_w, ffn_conv_b, ffn_w_out):
    B, S, D = x.shape
    T = B * S
    depth = pre_mix_norm.shape[0]
    n_mixers = 2
    hgrn_heads = D // HGRN_EXPAND
    x2d = x.reshape(T, D)
    hn = None
    tm_out = 1024
    for l in range(depth):
        j = l // n_mixers
        if l % n_mixers == 0:
            q, k, v = _mla_proj(x2d.reshape(B, S, D), positions, pre_mix_norm[l], mla_w_in[j],
                                mla_q_norm[j], mla_w_q_up[j], mla_kv_norm[j], mla_w_kv_up[j])
            o = _attention(q, k, v).reshape(T, -1)
            x2d, hn = _mix_out_call(
                _mla_out_kernel, [o], [pl.BlockSpec((tm_out, o.shape[1]), lambda i: (i, 0))],
                x2d, mla_w_out[j], post_mix_norm[l], pre_ffn_norm[l], tm_out, "mla_out")
        else:
            proj = _hgrn_in_proj(hn, hgrn_w_in[j], heads=hgrn_heads)
            of, ob = _hgrn_scan(proj, hgrn_lb_logits, batch=B, seq_len=S, layer_idx=l,
                                heads=hgrn_heads)
            hd = proj.shape[2]
            head_rows = pl.BlockSpec((hgrn_heads, tm_out, hd), lambda i: (0, i, 0))
            gate_rows = pl.BlockSpec((hgrn_heads, tm_out, hd), lambda i: (4, i, 0))
            x2d, hn = _mix_out_call(
                _hgrn_out_kernel,
                [of, ob, proj, hgrn_out_norm[j].reshape(hgrn_heads, 1, hd)],
                [head_rows, head_rows, gate_rows, _const_spec((hgrn_heads, 1, hd))],
                x2d, hgrn_w_out[j], post_mix_norm[l], pre_ffn_norm[l], tm_out, "hgrn_out")
        last = l == depth - 1
        g_next = pre_mix_norm[l] if last else pre_mix_norm[l + 1]
        outs = _conv_ffn(hn, x2d, S, ffn_w_in[l], ffn_conv_w[l], ffn_conv_b[l], ffn_w_out[l],
                         post_ffn_norm[l], g_next, emit_next=not last)
        if last:
            (x2d,) = outs
        else:
            x2d, hn = outs
    return x2d.reshape(B, S, D)
```

```python
import functools

import jax
import jax.numpy as jnp
import numpy as np
from jax import lax
from jax.experimental import pallas as pl
from jax.experimental.pallas import tpu as pltpu

F32 = jnp.float32
BF16 = jnp.bfloat16

EPS = 1e-6
LOG2_E = 1.4426950408889634
ROPE_THETA = 10000.0
MLA_NOPE = 128
MLA_ROPE = 64
MLA_V = 128
HGRN_EXPAND = 128
SCAN_CHUNK = 64
SCAN_FAST_MIN_HALF_DECAY = 2.0 ** -100
FFN_COL_CHUNK = 256

V7X_LANES = 128
V7X_SUBLANES = 8
V7X_BF16_ROWS = 16
V7X_VMEM_LIMIT = 56 * 1024 * 1024


def _rms(x, gain):
    ms = jnp.mean(x * x, axis=-1, keepdims=True)
    return x * lax.rsqrt(ms + EPS) * gain


def _sigmoid_pair(x):
    e = jnp.exp(-jnp.abs(x))
    r = 1.0 / (1.0 + e)
    er = e * r
    pos = x >= 0
    return jnp.where(pos, r, er), jnp.where(pos, er, r)


def _silu(x):
    return x / (1.0 + jnp.exp(-x))


def _dot(a, b):
    return jnp.dot(a, b, preferred_element_type=F32)


def _dot_nt(a, b):
    return lax.dot_general(a, b, (((1,), (1,)), ((), ())), preferred_element_type=F32)


def _dot_tn(a, b):
    return lax.dot_general(a, b, (((0,), (0,)), ((), ())), preferred_element_type=F32)


def _params(*sem):
    return pltpu.CompilerParams(dimension_semantics=sem, vmem_limit_bytes=V7X_VMEM_LIMIT)


def _const_spec(shape):
    nd = len(shape)
    return pl.BlockSpec(shape, lambda *_: (0,) * nd)


def _mla_proj_kernel(x_ref, pos_ref, g_pre_ref, w_in_ref, g_q_ref, w_q_ref, g_kv_ref,
                     w_kv_ref, invf_ref, sgn_ref, q_ref, k_ref, v_ref, *, q_lora, kv_lora, heads):
    hn = _rms(x_ref[...], g_pre_ref[...]).astype(BF16)
    proj = _dot(hn, w_in_ref[...])
    c_q = proj[:, :q_lora]
    c_kv = proj[:, q_lora:q_lora + kv_lora]
    kr = proj[:, q_lora + kv_lora:q_lora + kv_lora + MLA_ROPE]
    kr_sw = proj[:, q_lora + kv_lora + MLA_ROPE:]

    ang = pos_ref[...].astype(F32) * invf_ref[...]
    cos2 = jnp.cos(ang)
    sin2 = jnp.sin(ang) * sgn_ref[...]

    q_all = _dot(_rms(c_q, g_q_ref[...]).astype(BF16), w_q_ref[...])
    kv = _dot(_rms(c_kv, g_kv_ref[...]).astype(BF16), w_kv_ref[...])

    n_nope = heads * MLA_NOPE
    n_rope = heads * MLA_ROPE
    k_rot = (kr * cos2[:, :MLA_ROPE] + kr_sw * sin2[:, :MLA_ROPE]).astype(BF16)
    for pair in range(heads // 2):
        lo = n_nope + pair * V7X_LANES
        q_rot = (q_all[:, lo:lo + V7X_LANES] * cos2
                 + q_all[:, lo + n_rope:lo + n_rope + V7X_LANES] * sin2).astype(BF16)
        for sub in range(2):
            h = 2 * pair + sub
            q_ref[h, :, MLA_NOPE:] = q_rot[:, sub * MLA_ROPE:(sub + 1) * MLA_ROPE]
    for h in range(heads):
        q_ref[h, :, :MLA_NOPE] = q_all[:, h * MLA_NOPE:(h + 1) * MLA_NOPE].astype(BF16)
        k_ref[h, :, :MLA_NOPE] = kv[:, h * MLA_NOPE:(h + 1) * MLA_NOPE].astype(BF16)
        k_ref[h, :, MLA_NOPE:] = k_rot
        v_ref[h, :, :MLA_V] = kv[:, n_nope + h * MLA_V:n_nope + (h + 1) * MLA_V].astype(BF16)
        v_ref[h, :, MLA_V:] = jnp.ones((v_ref.shape[1], MLA_V), BF16)


def _mla_proj(x, positions, g_pre, w_in, g_q, w_q_up, g_kv, w_kv_up, *, tm=512):
    B, S, D = x.shape
    q_lora = g_q.shape[0]
    kv_lora = g_kv.shape[0]
    heads = w_q_up.shape[1] // (MLA_NOPE + MLA_ROPE)
    half = MLA_ROPE // 2
    qk_dim = MLA_NOPE + MLA_ROPE

    kr_lo = q_lora + kv_lora
    w_in_x = jnp.concatenate(
        [w_in, w_in[:, kr_lo + half:kr_lo + MLA_ROPE], w_in[:, kr_lo:kr_lo + half]], axis=1).astype(BF16)
    wq3 = w_q_up.reshape(q_lora, heads, qk_dim)
    rope = wq3[:, :, MLA_NOPE:]
    rope_sw = jnp.concatenate([rope[:, :, half:], rope[:, :, :half]], axis=-1)
    w_q_x = jnp.concatenate(
        [wq3[:, :, :MLA_NOPE].reshape(q_lora, -1), rope.reshape(q_lora, -1),
         rope_sw.reshape(q_lora, -1)], axis=1).astype(BF16)
    wkv3 = w_kv_up.reshape(kv_lora, heads, MLA_NOPE + MLA_V)
    w_kv_x = jnp.concatenate(
        [wkv3[:, :, :MLA_NOPE].reshape(kv_lora, -1), wkv3[:, :, MLA_NOPE:].reshape(kv_lora, -1)],
        axis=1).astype(BF16)

    inv_freq = 1.0 / (ROPE_THETA ** (jnp.arange(0, MLA_ROPE, 2, dtype=F32) / MLA_ROPE))
    invf = jnp.tile(inv_freq, V7X_LANES // half)[None, :]
    sgn = jnp.tile(jnp.concatenate([-jnp.ones((half,), F32), jnp.ones((half,), F32)]),
                   V7X_LANES // MLA_ROPE)[None, :]
    scale = float(qk_dim) ** -0.5 * LOG2_E

    kern = functools.partial(_mla_proj_kernel, q_lora=q_lora, kv_lora=kv_lora, heads=heads)
    head_out = lambda width: pl.BlockSpec((None, heads, tm, width), lambda b, i: (b, 0, i, 0))
    return pl.pallas_call(
        kern,
        grid=(B, S // tm),
        in_specs=[
            pl.BlockSpec((None, tm, D), lambda b, i: (b, i, 0)),
            pl.BlockSpec((None, tm, 1), lambda b, i: (b, i, 0)),
            _const_spec((1, D)),
            _const_spec(w_in_x.shape),
            _const_spec((1, q_lora)),
            _const_spec(w_q_x.shape),
            _const_spec((1, kv_lora)),
            _const_spec(w_kv_x.shape),
            _const_spec((1, V7X_LANES)),
            _const_spec((1, V7X_LANES)),
        ],
        out_specs=[head_out(qk_dim), head_out(qk_dim), head_out(2 * MLA_V)],
        out_shape=[
            jax.ShapeDtypeStruct((B, heads, S, qk_dim), BF16),
            jax.ShapeDtypeStruct((B, heads, S, qk_dim), BF16),
            jax.ShapeDtypeStruct((B, heads, S, 2 * MLA_V), BF16),
        ],
        compiler_params=_params("parallel", "parallel"),
        name="mla_proj",
    )(x, positions.reshape(B, S, 1), g_pre[None, :], w_in_x, (g_q * scale)[None, :], w_q_x,
      g_kv[None, :], w_kv_x, invf, sgn)


def _attn_kernel(q_ref, k_ref, v_ref, o_ref, *, tq):
    def q_tile(i, carry):
        r0 = pl.multiple_of(i * tq, tq)
        s = _dot_nt(q_ref[pl.ds(r0, tq), :], k_ref[...])
        p = jnp.exp2(s - jnp.max(s, axis=-1, keepdims=True))
        o = _dot(p.astype(BF16), v_ref[...])
        dv = o_ref.shape[1]
        o_ref[pl.ds(r0, tq), :] = (o[:, :dv] / o[:, dv:]).astype(o_ref.dtype)
        return carry

    lax.fori_loop(0, q_ref.shape[0] // tq, q_tile, 0, unroll=True)


def _attention(q, k, v, *, tq=256):
    B, H, S, dk = q.shape
    dv = v.shape[-1] // 2
    per_head = lambda width: pl.BlockSpec((None, None, S, width), lambda b, h: (b, h, 0, 0))
    return pl.pallas_call(
        functools.partial(_attn_kernel, tq=tq),
        grid=(B, H),
        in_specs=[per_head(dk), per_head(dk), per_head(2 * dv)],
        out_specs=pl.BlockSpec((None, S, dv), lambda b, h: (b, 0, h)),
        out_shape=jax.ShapeDtypeStruct((B, S, H * dv), BF16),
        compiler_params=_params("parallel", "parallel"),
        name="mla_attention",
    )(q, k, v)


def _mix_epilogue(a_bf16, x_ref, w_ref, g_post_ref, g_ffn_ref, x_out_ref, hn_out_ref):
    m = _dot(a_bf16, w_ref[...])
    x1 = x_ref[...] + _rms(m, g_post_ref[...])
    x_out_ref[...] = x1
    hn_out_ref[...] = _rms(x1, g_ffn_ref[...]).astype(BF16)


def _mla_out_kernel(o_ref, x_ref, w_ref, g_post_ref, g_ffn_ref, x_out_ref, hn_out_ref):
    _mix_epilogue(o_ref[...], x_ref, w_ref, g_post_ref, g_ffn_ref, x_out_ref, hn_out_ref)


def _hgrn_out_kernel(of_ref, ob_ref, gate_ref, g_head_ref, x_ref, w_ref, g_post_ref, g_ffn_ref,
                     x_out_ref, hn_out_ref):
    heads = of_ref.shape[0]
    cols = []
    for h in range(heads):
        o = of_ref[h].astype(F32) + ob_ref[h].astype(F32)
        cols.append((_rms(o, g_head_ref[h]) * gate_ref[h]).astype(BF16))
    _mix_epilogue(jnp.concatenate(cols, axis=1), x_ref, w_ref, g_post_ref, g_ffn_ref,
                  x_out_ref, hn_out_ref)


def _mix_out_call(kern, lead_args, lead_specs, x2d, w_out, g_post, g_ffn, tm, name):
    T, D = x2d.shape
    row = pl.BlockSpec((tm, D), lambda i: (i, 0))
    return pl.pallas_call(
        kern,
        grid=(T // tm,),
        in_specs=lead_specs + [row, _const_spec(w_out.shape), _const_spec((1, D)), _const_spec((1, D))],
        out_specs=[row, row],
        out_shape=[jax.ShapeDtypeStruct((T, D), F32), jax.ShapeDtypeStruct((T, D), BF16)],
        compiler_params=_params("parallel"),
        name=name,
    )(*lead_args, x2d, w_out.astype(BF16), g_post[None, :], g_ffn[None, :])


def _gelu_tanh(x):
    c = 0.7978845608028654
    return 0.5 * x * (1.0 + jnp.tanh(c * (x + 0.044715 * (x * x * x))))


def _ffn_kernel(hn_ref, hprev_ref, hnext_ref, x_ref, w_in_ref, cw_ref, cb_ref, wo_ref,
                g_post_ref, g_next_ref, x_out_ref, *rest, tiles_per_seq, emit_next):
    if emit_next:
        hn_out_ref, act_ref = rest
    else:
        (act_ref,) = rest
    i = pl.program_id(0)
    tm = hn_ref.shape[0]
    hn = hn_ref[...]
    halo = jnp.concatenate([hprev_ref[...], hnext_ref[...]], axis=0)
    t_in_seq = i % tiles_per_seq
    seq_start = t_in_seq == 0
    seq_end = t_in_seq == tiles_per_seq - 1
    rows = lax.broadcasted_iota(jnp.int32, (tm, 1), 0)

    nc, _, cw_cols = act_ref.shape
    d_ff = nc * cw_cols

    def col_chunk(c):
        cols = slice(c * cw_cols, (c + 1) * cw_cols)
        wg = w_in_ref[:, cols]
        g = _dot(hn, wg)
        gh = _dot(halo, wg)
        val = _dot(hn, w_in_ref[:, d_ff + c * cw_cols:d_ff + (c + 1) * cw_cols])
        g_before = jnp.where(seq_start, 0.0, gh[V7X_BF16_ROWS - 1:V7X_BF16_ROWS, :])
        g_after = jnp.where(seq_end, 0.0, gh[V7X_BF16_ROWS:V7X_BF16_ROWS + 1, :])
        g_m1 = jnp.where(rows == 0, g_before, pltpu.roll(g, 1, 0))
        g_p1 = jnp.where(rows == tm - 1, g_after, pltpu.roll(g, tm - 1, 0))
        cw = cw_ref[:, cols]
        conv = g_m1 * cw[0:1, :] + g * cw[1:2, :] + g_p1 * cw[2:3, :] + cb_ref[:, cols]
        act_ref[c] = (_gelu_tanh(conv) * val).astype(BF16)

    for c in range(nc):
        col_chunk(c)
    f = _dot(act_ref[0], wo_ref[:cw_cols, :])
    for c in range(1, nc):
        f = f + _dot(act_ref[c], wo_ref[c * cw_cols:(c + 1) * cw_cols, :])
    x2 = x_ref[...] + _rms(f, g_post_ref[...])
    x_out_ref[...] = x2
    if emit_next:
        hn_out_ref[...] = _rms(x2, g_next_ref[...]).astype(BF16)


def _conv_ffn(hn, x2d, seq_len, layer, w_in_all, conv_w, conv_b, w_out_all, g_post, g_next, *,
              emit_next, tm=512):
    T, D = x2d.shape
    d_ff = w_out_all.shape[1]
    cw = FFN_COL_CHUNK
    nc = d_ff // cw
    w_in16 = w_in_all.astype(BF16)
    wo = w_out_all.astype(BF16)
    layer_block = lambda shape: pl.BlockSpec((None,) + shape[1:], lambda i: (layer, 0, 0))
    cbt = conv_b[None, :]
    halo_rows = V7X_BF16_ROWS
    per_tile = tm // halo_rows
    n_halo = T // halo_rows
    row = pl.BlockSpec((tm, D), lambda i: (i, 0))
    out_specs = [row]
    out_shape = [jax.ShapeDtypeStruct((T, D), F32)]
    if emit_next:
        out_specs.append(row)
        out_shape.append(jax.ShapeDtypeStruct((T, D), BF16))
    kern = functools.partial(_ffn_kernel, tiles_per_seq=seq_len // tm, emit_next=emit_next)
    return pl.pallas_call(
        kern,
        grid=(T // tm,),
        in_specs=[
            row,
            pl.BlockSpec((halo_rows, D), lambda i: (jnp.maximum(i * per_tile - 1, 0), 0)),
            pl.BlockSpec((halo_rows, D), lambda i: (jnp.minimum((i + 1) * per_tile, n_halo - 1), 0)),
            row,
            layer_block(w_in16.shape), _const_spec(conv_w.shape),
            _const_spec(cbt.shape), layer_block(wo.shape),
            _const_spec((1, D)), _const_spec((1, D)),
        ],
        out_specs=out_specs,
        out_shape=out_shape,
        scratch_shapes=[pltpu.VMEM((nc, tm, cw), BF16)],
        compiler_params=_params("parallel"),
        name="conv_ffn",
    )(hn, hn, hn, x2d, w_in16, conv_w, cbt, wo, g_post[None, :], g_next[None, :])


def _hgrn_in_kernel(hn_ref, w_ref, o_ref, *, silu_scale, heads):
    res = _dot(hn_ref[...], w_ref[...])
    n_blocks = o_ref.shape[0]
    n_tiles = len(silu_scale) * heads // n_blocks
    n = pl.program_id(0)
    for cb in range(n_blocks):
        y = res[:, cb * V7X_LANES:(cb + 1) * V7X_LANES]
        per_tile = [silu_scale[(t * n_blocks + cb) // heads] for t in range(n_tiles)]
        out = y
        if any(s is not None for s in per_tile):
            act = _silu(y)
            for t, s in enumerate(per_tile):
                if s is not None:
                    out = jnp.where(n == t, act * s, out)
        o_ref[cb] = out


def _hgrn_in_proj(hn, w_in, *, heads, tm=1024, tn=2560):
    T, D = hn.shape
    N = w_in.shape[1]
    silu_scale = (float(HGRN_EXPAND) ** -0.5, None, None, None, 1.0)
    return pl.pallas_call(
        functools.partial(_hgrn_in_kernel, silu_scale=silu_scale, heads=heads),
        grid=(N // tn, T // tm),
        in_specs=[pl.BlockSpec((tm, D), lambda n, i: (i, 0)),
                  pl.BlockSpec((D, tn), lambda n, i: (0, n))],
        out_specs=pl.BlockSpec((tn // V7X_LANES, tm, V7X_LANES), lambda n, i: (n, i, 0)),
        out_shape=jax.ShapeDtypeStruct((N // V7X_LANES, T, V7X_LANES), F32),
        compiler_params=_params("parallel", "parallel"),
        name="hgrn_in_proj",
    )(hn, w_in.astype(BF16))


def _scan_gates(lbl_ref, q_ref, f_ref, d, h, layer_idx):
    depth = lbl_ref.shape[1]
    logits = [lbl_ref[d, i, h] for i in range(depth)]
    mx = functools.reduce(jnp.maximum, logits)
    es = [jnp.exp(l - mx) for l in logits]
    tot = functools.reduce(lambda a, b: a + b, es)
    picked = es[1:layer_idx + 1]
    lb = functools.reduce(lambda a, b: a + b, picked) / tot if picked else jnp.zeros_like(tot)
    one_m_lb = 1.0 - lb
    sig_f, sig_nf = _sigmoid_pair(f_ref[h])
    q = q_ref[h]
    return q, lb + one_m_lb * sig_f, one_m_lb * sig_nf


def _half_chunk_decay(f):
    nb = SCAN_CHUNK // V7X_SUBLANES
    out = []
    for half in range(2):
        bl = [f[V7X_SUBLANES * j:V7X_SUBLANES * (j + 1), :] for j in range(half * nb // 2, (half + 1) * nb // 2)]
        p = functools.reduce(lambda a, b: a * b, bl)
        for shift in (4, 2, 1):
            p = p * pltpu.roll(p, shift, 0)
        out.append(p)
    return out


def _scan_core(q, f, k, v, mask_ref, o_ref, st_ref, d, h, *, rev, fast):
    nb = SCAN_CHUNK // V7X_SUBLANES
    n_levels = mask_ref.shape[1] - 1

    sub = lax.broadcasted_iota(jnp.int32, (V7X_SUBLANES, V7X_LANES), 0)
    il = (V7X_SUBLANES - 1 - sub) if rev else sub

    def blocks(a):
        bl = [a[V7X_SUBLANES * j:V7X_SUBLANES * (j + 1), :] for j in range(nb)]
        return bl[::-1] if rev else bl

    def unblocks(bl):
        return jnp.concatenate(bl[::-1] if rev else bl, axis=0)

    def row(x, pos):
        u = (V7X_SUBLANES - 1 - pos) if rev else pos
        return jnp.broadcast_to(x[u:u + 1, :], x.shape)

    def at_prev(x):
        return pltpu.roll(x, (V7X_SUBLANES - 1) if rev else 1, 0)

    def at_next(x):
        return pltpu.roll(x, 1 if rev else (V7X_SUBLANES - 1), 0)

    qb, kb = blocks(q), blocks(k)
    hi1, hi2, hi4 = [(il & c) != 0 for c in (1, 2, 4)]

    def level_operands(F, G):
        qh = unblocks([a * b for a, b in zip(qb, F)]).astype(BF16)
        kh = unblocks([a * b for a, b in zip(kb, G)] if G is not None else kb).astype(BF16)
        return qh, kh

    last = V7X_SUBLANES - 1

    def next_prefix(F, lvl):
        if lvl == 0:
            return [x * jnp.where(hi1, at_prev(x), 1.0) for x in F]
        if lvl == 1:
            return [x * jnp.where(hi2, jnp.where(hi4, row(x, 5), row(x, 1)), 1.0) for x in F]
        if lvl == 2:
            return [x * jnp.where(hi4, row(x, 3), 1.0) for x in F]
        cb = 2 ** (lvl - 3)
        return [F[b] * row(F[(b & ~(2 * cb - 1)) + cb - 1], last) if b & cb else F[b]
                for b in range(nb)]

    def next_level(F, G, lvl):
        if lvl == 0:
            newG = [jnp.where(hi1, 1.0, at_next(x)) for x in F]
        elif lvl == 1:
            newG = [g * jnp.where(hi2, 1.0, jnp.where(hi4, row(x, 7), row(x, 3))) for x, g in zip(F, G)]
        elif lvl == 2:
            newG = [g * jnp.where(hi4, 1.0, row(x, 7)) for x, g in zip(F, G)]
        else:
            cb = 2 ** (lvl - 3)
            newG = [G[b] if b & cb else G[b] * row(F[(b & ~(2 * cb - 1)) + 2 * cb - 1], last)
                    for b in range(nb)]
        return next_prefix(F, lvl), newG

    F, G = blocks(f), None
    if fast:
        for lvl in range(n_levels - 1):
            F = next_prefix(F, lvl)
        half = nb // 2
        tot1, tot2 = row(F[half - 1], last), row(F[nb - 1], last)
        inv = [1.0 / x for x in F]
        inv_tot1 = 1.0 / tot1
        q_fac = [F[b] * inv_tot1 if b < half else F[b] for b in range(nb)]
        k_fac = [inv[b] * tot1 if b < half else inv[b] for b in range(nb)]
        qh = unblocks([a * b for a, b in zip(qb, q_fac)]).astype(BF16)
        kh = unblocks([a * b for a, b in zip(kb, k_fac)]).astype(BF16)
        s_acc = jnp.where(mask_ref[d, n_levels] != 0.0, _dot_nt(qh, kh), 0.0)
        tot = tot1 * tot2
        G = [inv[b] * (tot if b < half else tot2) for b in range(nb)]
        F = [F[b] if b < half else F[b] * tot1 for b in range(nb)]
    else:
        s_acc = None
        for lvl in range(n_levels):
            qh, kh = level_operands(F, G)
            term = _dot_nt(qh, kh) * mask_ref[d, lvl]
            s_acc = term if s_acc is None else s_acc + term
            F, G = next_level(F, G, lvl)

    q_dec, k_dec = level_operands(F, G)
    st = st_ref[d, h]
    v16 = v.astype(BF16)
    o = _dot(q_dec, st.astype(BF16)) + _dot(s_acc.astype(BF16), v16)
    if not fast:
        o = o + jnp.sum(q * k, axis=-1, keepdims=True) * v
    o_ref[h] = o.astype(o_ref.dtype)
    f_total = row(F[nb - 1], last)[0:1, :]
    decay_rows = jnp.broadcast_to(f_total, (st.shape[1], st.shape[0])).T
    st_ref[d, h] = st * decay_rows + _dot_tn(k_dec, v16)


def _scan_level_masks():
    n = SCAN_CHUNK
    pos = np.arange(n)
    n_levels = n.bit_length() - 1
    out = np.zeros((2, n_levels + 1, n, n), np.float32)
    for d in range(2):
        idx = (n - 1 - pos) if d == 1 else pos
        ti, si = idx[:, None], idx[None, :]
        lvl, c = 0, 1
        while c < n:
            out[d, lvl] = ((ti & c) != 0) & ((si & c) == 0) & ((ti // (2 * c)) == (si // (2 * c)))
            lvl, c = lvl + 1, 2 * c
        out[d, n_levels] = out[d, :n_levels].sum(axis=0) + np.eye(n, dtype=np.float32)
    return out


def _hgrn_scan_kernel(lbl_ref, mask_ref, qf_ref, ff_ref, vf_ref, qb_ref, fb_ref, vb_ref, of_ref,
                      ob_ref, st_ref, gate_ref, *, layer_idx):
    @pl.when(pl.program_id(1) == 0)
    def _():
        st_ref[...] = jnp.zeros_like(st_ref)

    heads = qf_ref.shape[0]
    dirs = ((qf_ref, ff_ref, vf_ref, of_ref), (qb_ref, fb_ref, vb_ref, ob_ref))

    def gate_head(h, weakest):
        for d, (q_ref, f_ref, _, _) in enumerate(dirs):
            q, f, k = _scan_gates(lbl_ref, q_ref, f_ref, d, h, layer_idx)
            gate_ref[0, d, h] = q
            gate_ref[1, d, h] = f
            gate_ref[2, d, h] = k
            for p in _half_chunk_decay(f):
                weakest = jnp.minimum(weakest, p)
        return weakest

    weakest = lax.fori_loop(0, heads, gate_head,
                            jnp.ones((V7X_SUBLANES, V7X_LANES), F32), unroll=heads)
    fast_ok = jnp.min(weakest) >= SCAN_FAST_MIN_HALF_DECAY

    def one(h, d, fast):
        _, _, v_ref, o_ref = dirs[d]
        _scan_core(gate_ref[0, d, h], gate_ref[1, d, h], gate_ref[2, d, h], v_ref[h], mask_ref,
                   o_ref, st_ref, d, h, rev=d == 1, fast=fast)

    @pl.when(fast_ok)
    def _():
        for h in range(heads):
            for d in range(2):
                one(h, d, True)

    @pl.when(jnp.logical_not(fast_ok))
    def _():
        def head(h, carry):
            for d in range(2):
                one(h, d, False)
            return carry
        lax.fori_loop(0, heads, head, 0, unroll=2)


def _hgrn_scan(proj, lb_logits, *, batch, seq_len, layer_idx, heads):
    T = proj.shape[1]
    hd = proj.shape[2]
    nchunk = seq_len // SCAN_CHUNK
    p5 = proj.reshape(5, heads, batch, seq_len, hd)
    depth = lb_logits.shape[1]
    lbl = lb_logits.reshape(2, depth, heads, 1, hd)
    masks = jnp.asarray(_scan_level_masks())

    def sec(section, backward):
        if backward:
            return pl.BlockSpec((None, heads, None, SCAN_CHUNK, hd),
                                lambda b, j: (section, 0, b, nchunk - 1 - j, 0))
        return pl.BlockSpec((None, heads, None, SCAN_CHUNK, hd), lambda b, j: (section, 0, b, j, 0))

    out_f = pl.BlockSpec((heads, None, SCAN_CHUNK, hd), lambda b, j: (0, b, j, 0))
    out_b = pl.BlockSpec((heads, None, SCAN_CHUNK, hd), lambda b, j: (0, b, nchunk - 1 - j, 0))
    o_sds = jax.ShapeDtypeStruct((heads, batch, seq_len, hd), BF16)
    of, ob = pl.pallas_call(
        functools.partial(_hgrn_scan_kernel, layer_idx=layer_idx),
        grid=(batch, nchunk),
        in_specs=[_const_spec(lbl.shape), _const_spec(masks.shape),
                  sec(0, False), sec(1, False), sec(3, False),
                  sec(0, True), sec(2, True), sec(3, True)],
        out_specs=[out_f, out_b],
        out_shape=[o_sds, o_sds],
        scratch_shapes=[pltpu.VMEM((2, heads, hd, hd), F32),
                        pltpu.VMEM((3, 2, heads, SCAN_CHUNK, hd), F32)],
        compiler_params=_params("arbitrary", "arbitrary"),
        name="hgrn_scan",
    )(lbl, masks, p5, p5, p5, p5, p5, p5)
    return of.reshape(heads, T, hd), ob.reshape(heads, T, hd)


def kernel(x, positions, pre_mix_norm, post_mix_norm, pre_ffn_norm, post_ffn_norm, mla_w_in, mla_q_norm, mla_w_q_up, mla_kv_norm, mla_w_kv_up, mla_w_out, hgrn_w_in, hgrn_lb_logits, hgrn_out_norm, hgrn_w_out, ffn_w_in, ffn_conv_w, ffn_conv_b, ffn_w_out):
    B, S, D = x.shape
    T = B * S
    depth = pre_mix_norm.shape[0]
    n_mixers = 2
    hgrn_heads = D // HGRN_EXPAND
    x2d = x.reshape(T, D)
    hn = None
    tm_out = 1024
    for l in range(depth):
        j = l // n_mixers
        if l % n_mixers == 0:
            q, k, v = _mla_proj(x2d.reshape(B, S, D), positions, pre_mix_norm[l], mla_w_in[j],
                                mla_q_norm[j], mla_w_q_up[j], mla_kv_norm[j], mla_w_kv_up[j])
            o = _attention(q, k, v).reshape(T, -1)
            x2d, hn = _mix_out_call(
                _mla_out_kernel, [o], [pl.BlockSpec((tm_out, o.shape[1]), lambda i: (i, 0))],
                x2d, mla_w_out[j], post_mix_norm[l], pre_ffn_norm[l], tm_out, "mla_out")
        else:
            proj = _hgrn_in_proj(hn, hgrn_w_in[j], heads=hgrn_heads)
            of, ob = _hgrn_scan(proj, hgrn_lb_logits, batch=B, seq_len=S, layer_idx=l,
                                heads=hgrn_heads)
            hd = proj.shape[2]
            head_rows = pl.BlockSpec((hgrn_heads, tm_out, hd), lambda i: (0, i, 0))
            gate_rows = pl.BlockSpec((hgrn_heads, tm_out, hd), lambda i: (4, i, 0))
            x2d, hn = _mix_out_call(
                _hgrn_out_kernel,
                [of, ob, proj, hgrn_out_norm[j].reshape(hgrn_heads, 1, hd)],
                [head_rows, head_rows, gate_rows, _const_spec((hgrn_heads, 1, hd))],
                x2d, hgrn_w_out[j], post_mix_norm[l], pre_ffn_norm[l], tm_out, "hgrn_out")
        last = l == depth - 1
        g_next = pre_mix_norm[l] if last else pre_mix_norm[l + 1]
        outs = _conv_ffn(hn, x2d, S, l, ffn_w_in, ffn_conv_w[l], ffn_conv_b[l], ffn_w_out,
                         post_ffn_norm[l], g_next, emit_next=not last)
        if last:
            (x2d,) = outs
        else:
            x2d, hn = outs
    return x2d.reshape(B, S, D)
```
